```python
import jax, jax.numpy as jnp
from jax import lax
import numpy as np

D_MODEL = 2048
BATCH = 4
SEQ = 4096
DEPTH = 4
DEC_BATCH = 8
DEC_SEQ = 32
PAST_LEN = 2048

CHUNK = 64
D_A = D_MODEL
N_HEADS_A = 4
DH_A = D_A // N_HEADS_A
D_B = D_MODEL // 2
POOL_WINDOWS = (2, 4, 8, 16)
N_POOL_GROUPS = 4
POOL_GROUP = D_B // N_POOL_GROUPS
POOL_HIST = 15
IN_SIZES = (D_A, D_A, D_A, D_A, D_A, 2 * N_HEADS_A, D_B, D_B, D_MODEL, D_MODEL)
N_IN = 5 * D_A + 2 * N_HEADS_A + 2 * D_B + 2 * D_MODEL
EPS = 1e-6

kernel_name = 'hybrid_mlstm_pool_stream_step'


def rmsnorm(x, w):
    x32 = x.astype(jnp.float32)
    y = x32 * lax.rsqrt(jnp.mean(x32 * x32, axis=-1, keepdims=True) + EPS)
    return (y * w.astype(jnp.float32)).astype(x.dtype)


def head_layernorm(h, w):
    mu = jnp.mean(h, axis=-1, keepdims=True)
    d = h - mu
    var = jnp.mean(d * d, axis=-1, keepdims=True)
    y = d * lax.rsqrt(var + EPS)
    return y.reshape(h.shape[0], h.shape[1], D_A) * w.astype(jnp.float32)


def split_cols(p):
    outs = []
    start = 0
    for s in IN_SIZES:
        outs.append(p[..., start:start + s])
        start += s
    return outs


def mlstm_chunk(q, k, v, ig, lf, c0, n0, m0):
    L = q.shape[2]
    b = jnp.cumsum(lf, axis=-1)
    causal = jnp.tril(jnp.ones((L, L), dtype=bool))
    logw = jnp.where(causal, b[..., :, None] - b[..., None, :] + ig[..., None, :], -jnp.inf)
    inter = b + m0[..., None]
    m = jnp.maximum(inter, jnp.max(logw, axis=-1))
    w_intra = jnp.exp(logw - m[..., None])
    a_inter = jnp.exp(inter - m)
    s = jnp.einsum('bhtd,bhsd->bhts', q, k) * w_intra
    num = a_inter[..., None] * jnp.einsum('bhtd,bhde->bhte', q, c0) + jnp.einsum('bhts,bhse->bhte', s, v)
    den = a_inter * jnp.einsum('bhtd,bhd->bht', q, n0) + jnp.sum(s, axis=-1)
    h = num / jnp.maximum(jnp.abs(den), jnp.exp(-m))[..., None]
    m_end = m[..., -1]
    w_end = jnp.exp(b[..., -1:] - b + ig - m_end[..., None])
    decay = jnp.exp(b[..., -1] + m0 - m_end)
    c1 = decay[..., None, None] * c0 + jnp.einsum('bhs,bhsd,bhse->bhde', w_end, k, v)
    n1 = decay[..., None] * n0 + jnp.einsum('bhs,bhsd->bhd', w_end, k)
    return h, c1, n1, m_end


def mlstm_sequence(q, k, v, ig, lf, c0, n0, m0):
    B, H, L, DH = q.shape
    if L <= CHUNK:
        return mlstm_chunk(q, k, v, ig, lf, c0, n0, m0)
    nc = L // CHUNK
    qc = jnp.moveaxis(q.reshape(B, H, nc, CHUNK, DH), 2, 0)
    kc = jnp.moveaxis(k.reshape(B, H, nc, CHUNK, DH), 2, 0)
    vc = jnp.moveaxis(v.reshape(B, H, nc, CHUNK, DH), 2, 0)
    igc = jnp.moveaxis(ig.reshape(B, H, nc, CHUNK), 2, 0)
    lfc = jnp.moveaxis(lf.reshape(B, H, nc, CHUNK), 2, 0)

    def body(carry, xs):
        c, n, m = carry
        qq, kk, vv, ii, ff = xs
        h, c, n, m = mlstm_chunk(qq, kk, vv, ii, ff, c, n, m)
        return (c, n, m), h

    (c1, n1, m1), hs = lax.scan(body, (c0, n0, m0), (qc, kc, vc, igc, lfc))
    h = jnp.moveaxis(hs, 0, 2).reshape(B, H, L, DH)
    return h, c1, n1, m1


def pool_mixer(u, hist, pos0, w_mix, scale):
    B, L, _ = u.shape
    full = jnp.concatenate([hist.astype(jnp.float32), u.astype(jnp.float32)], axis=1)
    cs = jnp.concatenate([jnp.zeros((B, 1, D_B), jnp.float32), jnp.cumsum(full, axis=1)], axis=1)
    end = cs[:, POOL_HIST + 1:]
    pos = pos0 + jnp.arange(L) + 1
    outs = []
    for g, w in enumerate(POOL_WINDOWS):
        sl = slice(g * POOL_GROUP, (g + 1) * POOL_GROUP)
        start = cs[:, POOL_HIST + 1 - w:POOL_HIST + 1 - w + L, sl]
        cnt = jnp.minimum(w, pos).astype(jnp.float32)
        outs.append((end[..., sl] - start) / cnt[None, :, None])
    pooled = jnp.concatenate(outs, axis=-1)
    d = (pooled - u.astype(jnp.float32)).reshape(B, L, N_POOL_GROUPS, POOL_GROUP)
    mixed = jnp.einsum('blgc,gcd->blgd', d, w_mix.astype(jnp.float32)).reshape(B, L, D_B)
    return mixed * scale.astype(jnp.float32), full[:, -POOL_HIST:]


def mixer_layer(x, c, c0, n0, m0, hist, pos0, norm_w, w_ada, b_ada, w_in, b_if, head_norm_w,
                w_pool_mix, pool_scale, w_branch_a, w_branch_b, w_out):
    B, L, _ = x.shape
    mod = jnp.einsum('bd,de->be', jax.nn.silu(c), w_ada) + b_ada
    shift, scl, gate = jnp.split(mod, 3, axis=-1)
    h = rmsnorm(x, norm_w) * (1 + scl[:, None, :]) + shift[:, None, :]
    proj = jnp.einsum('bld,de->ble', h, w_in)
    q, k, v, o, z_a, ifg, u, z_b, g_a, g_b = split_cols(proj)

    def heads(t):
        return t.reshape(B, L, N_HEADS_A, DH_A).transpose(0, 2, 1, 3).astype(jnp.float32)

    ifg = ifg.astype(jnp.float32) + b_if.astype(jnp.float32)
    ig = ifg[..., :N_HEADS_A].transpose(0, 2, 1)
    lf = jax.nn.log_sigmoid(ifg[..., N_HEADS_A:]).transpose(0, 2, 1)
    hm, c1, n1, m1 = mlstm_sequence(heads(q), heads(k) * (DH_A ** -0.5), heads(v), ig, lf, c0, n0, m0)
    hm = jax.nn.sigmoid(o.astype(jnp.float32)).reshape(B, L, N_HEADS_A, DH_A) * hm.transpose(0, 2, 1, 3)
    a = (jax.nn.silu(z_a.astype(jnp.float32)) * head_layernorm(hm, head_norm_w)).astype(x.dtype)

    pb, new_hist = pool_mixer(u, hist, pos0, w_pool_mix, pool_scale)
    bb = (jax.nn.silu(z_b.astype(jnp.float32)) * pb).astype(x.dtype)

    merged = (jax.nn.sigmoid(g_a) * jnp.einsum('ble,ed->bld', a, w_branch_a)
              + jax.nn.sigmoid(g_b) * jnp.einsum('ble,ed->bld', bb, w_branch_b))
    y = x + gate[:, None, :] * jnp.einsum('bld,de->ble', merged, w_out)
    return y, c1, n1, m1, new_hist


def setup_inputs(seed: int = 0) -> dict:
    key = jax.random.key(seed)
    ks = jax.random.split(key, 24)
    f32 = jnp.float32
    nrm = lambda k, s: jax.random.normal(k, s, f32)
    b_if = jnp.concatenate([0.1 * nrm(ks[14], (DEPTH, N_HEADS_A)),
                            jnp.linspace(3.0, 6.0, N_HEADS_A)[None, :] + 0.1 * nrm(ks[15], (DEPTH, N_HEADS_A))], axis=-1)
    return {
        'x_prompt': nrm(ks[0], (BATCH, SEQ, D_MODEL)),
        'x_sample': nrm(ks[1], (DEC_BATCH, DEC_SEQ, D_MODEL)),
        'c_prompt': nrm(ks[2], (BATCH, D_MODEL)),
        'c_sample': nrm(ks[3], (DEC_BATCH, D_MODEL)),
        'state_C': 0.5 * DH_A ** -0.5 * nrm(ks[4], (DEPTH, DEC_BATCH, N_HEADS_A, DH_A, DH_A)),
        'state_n': 0.1 * nrm(ks[5], (DEPTH, DEC_BATCH, N_HEADS_A, DH_A)),
        'state_m': 0.5 * nrm(ks[6], (DEPTH, DEC_BATCH, N_HEADS_A)),
        'state_pool': nrm(ks[7], (DEPTH, DEC_BATCH, POOL_HIST, D_B)),
        'norm_w': 1.0 + 0.02 * nrm(ks[8], (DEPTH, D_MODEL)),
        'w_ada': 0.5 * D_MODEL ** -0.5 * nrm(ks[9], (DEPTH, D_MODEL, 3 * D_MODEL)),
        'b_ada': 0.02 * nrm(ks[10], (DEPTH, 3 * D_MODEL)),
        'w_in': D_MODEL ** -0.5 * nrm(ks[11], (DEPTH, D_MODEL, N_IN)),
        'b_if': b_if,
        'head_norm_w': 1.0 + 0.02 * nrm(ks[12], (DEPTH, D_A)),
        'w_pool_mix': POOL_GROUP ** -0.5 * nrm(ks[13], (DEPTH, N_POOL_GROUPS, POOL_GROUP, POOL_GROUP)),
        'pool_scale': 1.0 + 0.1 * nrm(ks[16], (DEPTH, D_B)),
        'w_branch_a': D_A ** -0.5 * nrm(ks[17], (DEPTH, D_A, D_MODEL)),
        'w_branch_b': D_B ** -0.5 * nrm(ks[18], (DEPTH, D_B, D_MODEL)),
        'w_out': D_MODEL ** -0.5 * nrm(ks[19], (DEPTH, D_MODEL, D_MODEL)),
        'final_norm_w': 1.0 + 0.02 * nrm(ks[20], (D_MODEL,)),
    }


def reference(x_prompt, x_sample, c_prompt, c_sample, state_C, state_n, state_m, state_pool,
              norm_w, w_ada, b_ada, w_in, b_if, head_norm_w, w_pool_mix, pool_scale,
              w_branch_a, w_branch_b, w_out, final_norm_w):
    f32 = jnp.float32
    bp = x_prompt.shape[0]
    yp = x_prompt
    ys = x_sample
    cp_l, np_l, mp_l, hp_l = [], [], [], []
    cs_l, ns_l, ms_l, hs_l = [], [], [], []
    for l in range(DEPTH):
        params = (norm_w[l], w_ada[l], b_ada[l], w_in[l], b_if[l], head_norm_w[l],
                  w_pool_mix[l], pool_scale[l], w_branch_a[l], w_branch_b[l], w_out[l])
        yp, c1, n1, m1, h1 = mixer_layer(
            yp, c_prompt,
            jnp.zeros((bp, N_HEADS_A, DH_A, DH_A), f32), jnp.zeros((bp, N_HEADS_A, DH_A), f32),
            jnp.zeros((bp, N_HEADS_A), f32), jnp.zeros((bp, POOL_HIST, D_B), x_prompt.dtype), 0, *params)
        cp_l.append(c1); np_l.append(n1); mp_l.append(m1); hp_l.append(h1)
        ys, c2, n2, m2, h2 = mixer_layer(
            ys, c_sample, state_C[l].astype(f32), state_n[l].astype(f32), state_m[l].astype(f32),
            state_pool[l], PAST_LEN, *params)
        cs_l.append(c2); ns_l.append(n2); ms_l.append(m2); hs_l.append(h2)
    y_prompt = rmsnorm(yp, final_norm_w)
    y_sample = rmsnorm(ys, final_norm_w)
    pdt = x_prompt.dtype
    new_C_prompt = jnp.stack(cp_l).astype(pdt)
    new_n_prompt = jnp.stack(np_l).astype(pdt)
    new_m_prompt = jnp.stack(mp_l).astype(pdt)
    new_pool_prompt = jnp.stack(hp_l).astype(pdt)
    new_C_sample = jnp.stack(cs_l).astype(state_C.dtype)
    new_n_sample = jnp.stack(ns_l).astype(state_n.dtype)
    new_m_sample = jnp.stack(ms_l).astype(state_m.dtype)
    new_pool_sample = jnp.stack(hs_l).astype(state_pool.dtype)
    return (y_prompt, y_sample, new_C_prompt, new_n_prompt, new_m_prompt, new_pool_prompt,
            new_C_sample, new_n_sample, new_m_sample, new_pool_sample)
```

```python
import functools

import jax
import jax.numpy as jnp
from jax import lax
from jax.experimental import pallas as pl
from jax.experimental.pallas import tpu as pltpu

D_MODEL = 2048
N_HEADS = 4
DH = D_MODEL // N_HEADS
D_B = D_MODEL // 2
POOL_WINDOWS = (2, 4, 8, 16)
POOL_GROUP = D_B // len(POOL_WINDOWS)
POOL_HIST = 15
HIST_ROWS = 16
PAST_LEN = 2048
EPS = 1e-6
NEG = -1e30

LANES = 128
SUBLANES = 8
VMEM_LIMIT = 60 * 1024 * 1024

BF = jnp.bfloat16
F32 = jnp.float32

_dot = functools.partial(jnp.dot, preferred_element_type=F32)


def _dot_nt(a, b):
    return lax.dot_general(a, b, (((1,), (1,)), ((), ())), preferred_element_type=F32)


def _dot_tn(a, b):
    return lax.dot_general(a, b, (((0,), (0,)), ((), ())), preferred_element_type=F32)


def _params(n_axes):
    return pltpu.CompilerParams(dimension_semantics=("arbitrary",) * n_axes,
                                vmem_limit_bytes=VMEM_LIMIT)


def _const_spec(shape):
    nd = len(shape)
    return pl.BlockSpec(shape, lambda *_: (0,) * nd, pipeline_mode=pl.Buffered(1))


def _silu(x):
    return x * jax.nn.sigmoid(x)


def _log_sigmoid(x):
    return jnp.minimum(x, 0.0) - jnp.log(1.0 + jnp.exp(-jnp.abs(x)))


def _mod_kernel(c_ref, w_ref, b_ref, o_ref):
    sc = _silu(c_ref[...]).astype(BF)
    o_ref[0] = _dot(sc, w_ref[0].astype(BF)) + b_ref[0]


def _modulation(c_all, w_ada, b_ada):
    depth, _, n = w_ada.shape
    tn = 1024
    return pl.pallas_call(
        _mod_kernel,
        grid=(depth, n // tn),
        in_specs=[pl.BlockSpec(c_all.shape, lambda l, j: (0, 0)),
                  pl.BlockSpec((1, D_MODEL, tn), lambda l, j: (l, 0, j)),
                  pl.BlockSpec((1, 1, tn), lambda l, j: (l, 0, j))],
        out_specs=pl.BlockSpec((1, c_all.shape[0], tn), lambda l, j: (l, 0, j)),
        out_shape=jax.ShapeDtypeStruct((depth, c_all.shape[0], n), F32),
        compiler_params=_params(2),
        name="modulation",
    )(c_all, w_ada, b_ada.reshape(depth, 1, n))


def _norm_mod(y, nw, scl, shift):
    r = lax.rsqrt(jnp.mean(y * y, axis=-1, keepdims=True) + EPS)
    return (y * r * nw) * (1.0 + scl) + shift


def _h_kernel(x_ref, nw_ref, shift_ref, scl_ref, o_ref):
    o_ref[...] = _norm_mod(x_ref[...], nw_ref[...], scl_ref[0], shift_ref[0]).astype(BF)


def _first_h(x2, nw, mod_rows, mod_base, tiles_per_seq, t):
    rows = x2.shape[0]

    def mod_spec(kind):
        return pl.BlockSpec((1, 1, D_MODEL), lambda i: (mod_base + (i // tiles_per_seq) * 3 + kind, 0, 0))

    return pl.pallas_call(
        _h_kernel,
        grid=(rows // t,),
        in_specs=[pl.BlockSpec((t, D_MODEL), lambda i: (i, 0)),
                  pl.BlockSpec((1, D_MODEL), lambda i: (0, 0)),
                  mod_spec(0), mod_spec(1)],
        out_specs=pl.BlockSpec((t, D_MODEL), lambda i: (i, 0)),
        out_shape=jax.ShapeDtypeStruct((rows, D_MODEL), BF),
        compiler_params=_params(1),
        name="first_h",
    )(x2, nw, mod_rows, mod_rows)


def _mlstm_kernel(*refs, t, valid, has_state):
    if has_state:
        (bif_ref, h_ref, wq_ref, wk_ref, wv_ref, wo_ref, wz_ref, wifc_ref, wifr_ref, hnw_ref,
         c0_ref, n0_ref, m0_ref, a_ref, c_ref, n_ref, m_ref) = refs
    else:
        (bif_ref, h_ref, wq_ref, wk_ref, wv_ref, wo_ref, wz_ref, wifc_ref, wifr_ref, hnw_ref,
         a_ref, c_ref, n_ref, m_ref) = refs
    hd = pl.program_id(0)
    ck = pl.program_id(2)

    @pl.when(ck == 0)
    def _init():
        if has_state:
            c_ref[0, 0] = c0_ref[0, 0]
            n_ref[0, 0] = n0_ref[0, 0]
            m_ref[0, 0] = m0_ref[0, 0]
        else:
            c_ref[0, 0] = jnp.zeros((DH, DH), F32)
            n_ref[0, 0] = jnp.zeros((1, DH), F32)
            m_ref[0, 0] = jnp.zeros((1, LANES), F32)

    hb = h_ref[...]
    q = _dot(hb, wq_ref[...])
    k = _dot(hb, wk_ref[...]) * (DH ** -0.5)
    v = _dot(hb, wv_ref[...])
    qb = q.astype(BF)
    kb = k.astype(BF)
    vb = v.astype(BF)

    b_i = bif_ref[hd]
    b_f = bif_ref[N_HEADS + hd]
    pre_c = _dot(hb, wifc_ref[0])
    ig_c = pre_c[:, :LANES] + b_i
    lf_c = _log_sigmoid(pre_c[:, LANES:] + b_f)
    pre_r = _dot_nt(wifr_ref[0], hb)
    ig_r = pre_r[0:1, :] + b_i
    lf_r = _log_sigmoid(pre_r[SUBLANES:SUBLANES + 1, :] + b_f)

    row = lax.broadcasted_iota(jnp.int32, (t, t), 0)
    col = lax.broadcasted_iota(jnp.int32, (t, t), 1)
    causal = col <= row
    lf_cols = jnp.concatenate([lf_c] * (t // LANES), axis=1)
    b_c = jnp.sum(jnp.where(causal, lf_r, 0.0), axis=1, keepdims=True)
    b_r = jnp.sum(jnp.where(row <= col, lf_cols, 0.0), axis=0, keepdims=True)
    g_r = ig_r - b_r
    g_c = ig_c[:, 0:1] - b_c
    m0 = m_ref[0, 0][:, 0:1]
    big_m = jnp.maximum(jnp.max(jnp.where(causal, g_r, NEG), axis=1, keepdims=True), m0)
    w_intra = jnp.exp(jnp.where(causal, g_r - big_m, NEG))
    a_inter = jnp.exp(m0 - big_m)

    s = _dot_nt(qb, kb) * w_intra
    c_old = c_ref[0, 0]
    n_old = n_ref[0, 0]
    num = a_inter * _dot(qb, c_old.astype(BF)) + _dot(s.astype(BF), vb)
    den = a_inter * jnp.sum(q * n_old, axis=1, keepdims=True) + jnp.sum(s, axis=1, keepdims=True)
    hm = num * (1.0 / jnp.maximum(jnp.abs(den), jnp.exp(-(b_c + big_m))))

    last = valid - 1
    m_last = big_m[last:last + 1, :]
    b_last = b_c[last:last + 1, :]
    w_end = jnp.exp(g_c - m_last)
    if valid < t:
        w_end = jnp.where(lax.broadcasted_iota(jnp.int32, (t, 1), 0) < valid, w_end, 0.0)
    decay = jnp.exp(m0 - m_last)
    kw = k * w_end
    c_ref[0, 0] = decay * c_old + _dot_tn(kw.astype(BF), vb)
    n_ref[0, 0] = decay * n_old + jnp.sum(kw, axis=0, keepdims=True)
    m_ref[0, 0] = jnp.broadcast_to(b_last + m_last, (1, LANES))

    o = _dot(hb, wo_ref[...])
    z = _dot(hb, wz_ref[...])
    og = jax.nn.sigmoid(o) * hm
    mu = jnp.mean(og, axis=1, keepdims=True)
    dev = og - mu
    var = jnp.mean(dev * dev, axis=1, keepdims=True)
    yn = dev * lax.rsqrt(var + EPS) * hnw_ref[...]
    a_ref[...] = (_silu(z) * yn).astype(BF)


def _mlstm(h2, w5, wif_c, wif_r, b_if, hnw, state, *, batch, seq_rows, t, valid):
    nck = seq_rows // t
    has_state = state is not None
    grid = (N_HEADS, batch, nck)

    def w_spec(group):
        return pl.BlockSpec((D_MODEL, DH), lambda hd, b, c: (0, group * N_HEADS + hd))

    in_specs = [pl.BlockSpec(memory_space=pltpu.SMEM),
                pl.BlockSpec((t, D_MODEL), lambda hd, b, c: (b * nck + c, 0)),
                w_spec(0), w_spec(1), w_spec(2), w_spec(3), w_spec(4),
                pl.BlockSpec((1, D_MODEL, 2 * LANES), lambda hd, b, c: (hd, 0, 0)),
                pl.BlockSpec((1, 2 * SUBLANES, D_MODEL), lambda hd, b, c: (hd, 0, 0)),
                pl.BlockSpec((1, DH), lambda hd, b, c: (0, hd))]
    args = [b_if, h2, w5, w5, w5, w5, w5, wif_c, wif_r, hnw]
    state_specs = [pl.BlockSpec((1, 1, DH, DH), lambda hd, b, c: (b, hd, 0, 0)),
                   pl.BlockSpec((1, 1, 1, DH), lambda hd, b, c: (b, hd, 0, 0)),
                   pl.BlockSpec((1, 1, 1, LANES), lambda hd, b, c: (b, hd, 0, 0))]
    if has_state:
        in_specs += state_specs
        args += list(state)
    rows = batch * seq_rows
    return pl.pallas_call(
        functools.partial(_mlstm_kernel, t=t, valid=valid, has_state=has_state),
        grid=grid,
        in_specs=in_specs,
        out_specs=[pl.BlockSpec((t, DH), lambda hd, b, c: (b * nck + c, hd))] + state_specs,
        out_shape=[jax.ShapeDtypeStruct((rows, D_MODEL), BF),
                   jax.ShapeDtypeStruct((batch, N_HEADS, DH, DH), F32),
                   jax.ShapeDtypeStruct((batch, N_HEADS, 1, DH), F32),
                   jax.ShapeDtypeStruct((batch, N_HEADS, 1, LANES), F32)],
        compiler_params=_params(3),
        name="mlstm",
    )(*args)


def _pool_kernel(*refs, t, pos0, has_state):
    if has_state:
        (h_ref, wu_ref, wz_ref, wga_ref, wgb_ref, wmix_ref, ps_ref, hist0_ref,
         bb_ref, sga_ref, sgb_ref, hist_ref, uext) = refs
    else:
        (h_ref, wu_ref, wz_ref, wga_ref, wgb_ref, wmix_ref, ps_ref,
         bb_ref, sga_ref, sgb_ref, hist_ref, uext) = refs
    i = pl.program_id(1)

    @pl.when(i == 0)
    def _first():
        if has_state:
            uext[0:HIST_ROWS, :] = hist0_ref[0]
        else:
            uext[0:HIST_ROWS, :] = jnp.zeros((HIST_ROWS, D_B), F32)

    @pl.when(i > 0)
    def _carry():
        uext[0:HIST_ROWS, :] = uext[t:t + HIST_ROWS, :]

    hb = h_ref[...]
    u = _dot(hb, wu_ref[...])
    uext[HIST_ROWS:HIST_ROWS + t, :] = u
    zb = _dot(hb, wz_ref[...])
    sga_ref[...] = jax.nn.sigmoid(_dot(hb, wga_ref[...]))
    sgb_ref[...] = jax.nn.sigmoid(_dot(hb, wgb_ref[...]))

    pos = lax.broadcasted_iota(jnp.int32, (t, 1), 0) + (i * t + pos0 + 1)
    for g, w in enumerate(POOL_WINDOWS):
        sl = slice(g * POOL_GROUP, (g + 1) * POOL_GROUP)
        acc = uext[:, sl]
        shift = 1
        while shift < w:
            acc = acc + pltpu.roll(acc, shift, axis=0)
            shift *= 2
        cnt = jnp.minimum(pos, w).astype(F32)
        pooled = acc[HIST_ROWS:, :] / cnt
        dlt = (pooled - u[:, sl]).astype(BF)
        mixed = _dot(dlt, wmix_ref[g]) * ps_ref[:, sl]
        bb_ref[:, sl] = (_silu(zb[:, sl]) * mixed).astype(BF)

    @pl.when(i == pl.num_programs(1) - 1)
    def _hist():
        hist_ref[0] = uext[t:t + HIST_ROWS, :]


def _pool(h2, wu, wz, wga, wgb, wmix, ps, hist0, *, batch, seq_rows, t, pos0):
    nt = seq_rows // t
    has_state = hist0 is not None
    rows = batch * seq_rows
    in_specs = [pl.BlockSpec((t, D_MODEL), lambda b, i: (b * nt + i, 0)),
                _const_spec(wu.shape), _const_spec(wz.shape), _const_spec(wga.shape),
                _const_spec(wgb.shape), _const_spec(wmix.shape), _const_spec(ps.shape)]
    args = [h2, wu, wz, wga, wgb, wmix, ps]
    hist_spec = pl.BlockSpec((1, HIST_ROWS, D_B), lambda b, i: (b, 0, 0))
    if has_state:
        in_specs.append(hist_spec)
        args.append(hist0)
    return pl.pallas_call(
        functools.partial(_pool_kernel, t=t, pos0=pos0, has_state=has_state),
        grid=(batch, nt),
        in_specs=in_specs,
        out_specs=[pl.BlockSpec((t, D_B), lambda b, i: (b * nt + i, 0)),
                   pl.BlockSpec((t, D_MODEL), lambda b, i: (b * nt + i, 0)),
                   pl.BlockSpec((t, D_MODEL), lambda b, i: (b * nt + i, 0)),
                   hist_spec],
        out_shape=[jax.ShapeDtypeStruct((rows, D_B), BF),
                   jax.ShapeDtypeStruct((rows, D_MODEL), F32),
                   jax.ShapeDtypeStruct((rows, D_MODEL), F32),
                   jax.ShapeDtypeStruct((batch, HIST_ROWS, D_B), F32)],
        scratch_shapes=[pltpu.VMEM((HIST_ROWS + t, D_B), F32)],
        compiler_params=_params(2),
        name="pool",
    )(*args)


def _out_kernel(*refs, last):
    if last:
        (x_ref, a_ref, bb_ref, sga_ref, sgb_ref, gate_ref, wa_ref, wb_ref, wo_ref, nw_ref,
         y_ref) = refs
    else:
        (x_ref, a_ref, bb_ref, sga_ref, sgb_ref, gate_ref, wa_ref, wb_ref, wo_ref, nw_ref,
         shift_ref, scl_ref, y_ref, hn_ref) = refs
    br_a = _dot(a_ref[...], wa_ref[...])
    br_b = _dot(bb_ref[...], wb_ref[...])
    merged = (sga_ref[...] * br_a + sgb_ref[...] * br_b).astype(BF)
    y = x_ref[...] + gate_ref[0] * _dot(merged, wo_ref[...])
    if last:
        r = lax.rsqrt(jnp.mean(y * y, axis=-1, keepdims=True) + EPS)
        y_ref[...] = y * r * nw_ref[...]
    else:
        y_ref[...] = y
        hn_ref[...] = _norm_mod(y, nw_ref[...], scl_ref[0], shift_ref[0]).astype(BF)


def _out(x2, a2, bb2, sga, sgb, wa, wb, wo, nw, mod_rows, mod_base, next_mod_base, *,
         tiles_per_seq, t, last):
    rows = x2.shape[0]

    def mod_spec(base, kind):
        return pl.BlockSpec((1, 1, D_MODEL), lambda i: (base + (i // tiles_per_seq) * 3 + kind, 0, 0))

    def row_spec(width):
        return pl.BlockSpec((t, width), lambda i: (i, 0))

    in_specs = [row_spec(D_MODEL), row_spec(D_MODEL), row_spec(D_B), row_spec(D_MODEL), row_spec(D_MODEL),
                mod_spec(mod_base, 2),
                _const_spec(wa.shape), _const_spec(wb.shape), _const_spec(wo.shape),
                pl.BlockSpec((1, D_MODEL), lambda i: (0, 0))]
    args = [x2, a2, bb2, sga, sgb, mod_rows, wa, wb, wo, nw]
    if last:
        out_specs = [row_spec(D_MODEL)]
        out_shape = [jax.ShapeDtypeStruct((rows, D_MODEL), F32)]
    else:
        in_specs += [mod_spec(next_mod_base, 0), mod_spec(next_mod_base, 1)]
        args += [mod_rows, mod_rows]
        out_specs = [row_spec(D_MODEL), row_spec(D_MODEL)]
        out_shape = [jax.ShapeDtypeStruct((rows, D_MODEL), F32),
                     jax.ShapeDtypeStruct((rows, D_MODEL), BF)]
    return pl.pallas_call(
        functools.partial(_out_kernel, last=last),
        grid=(rows // t,),
        in_specs=in_specs,
        out_specs=out_specs,
        out_shape=out_shape,
        compiler_params=_params(1),
        name="out",
    )(*args)


MOD_ROWS = 16


def kernel(x_prompt, x_sample, c_prompt, c_sample, state_C, state_n, state_m, state_pool,
           norm_w, w_ada, b_ada, w_in, b_if, head_norm_w, w_pool_mix, pool_scale,
           w_branch_a, w_branch_b, w_out, final_norm_w):
    depth = w_in.shape[0]
    bp, lp, _ = x_prompt.shape
    bs, ls, _ = x_sample.shape
    assert bp + bs <= MOD_ROWS and lp >= POOL_HIST and ls >= POOL_HIST
    tp = 256
    ts_pad = LANES
    assert lp % tp == 0 and ls <= ts_pad

    c_all = jnp.zeros((MOD_ROWS, D_MODEL), F32).at[:bp].set(c_prompt).at[bp:bp + bs].set(c_sample)
    mod = _modulation(c_all, w_ada, b_ada)
    mod_rows = mod.reshape(depth * MOD_ROWS * 3, 1, D_MODEL)

    def mod_base(l, sample):
        return (l * MOD_ROWS + (bp if sample else 0)) * 3

    xp = x_prompt.reshape(bp * lp, D_MODEL)
    xs = x_sample.reshape(bs * ls, D_MODEL)
    nw0 = norm_w[0].reshape(1, D_MODEL)
    hp = _first_h(xp, nw0, mod_rows, mod_base(0, False), lp // tp, tp)
    hs = _first_h(xs, nw0, mod_rows, mod_base(0, True), 1, ls)

    outs = {k: [] for k in ("cp", "np", "mp", "pp", "cs", "ns", "ms", "ps")}
    n5 = 5 * D_MODEL
    for l in range(depth):
        wl = w_in[l]
        w5 = wl[:, :n5].astype(BF)
        w_if = wl[:, n5:n5 + 2 * N_HEADS]
        wi = w_if[:, :N_HEADS].T
        wf = w_if[:, N_HEADS:].T
        wif_c = jnp.concatenate([jnp.broadcast_to(wi[:, :, None], (N_HEADS, D_MODEL, LANES)),
                                 jnp.broadcast_to(wf[:, :, None], (N_HEADS, D_MODEL, LANES))], axis=2).astype(BF)
        wif_r = jnp.concatenate([jnp.broadcast_to(wi[:, None, :], (N_HEADS, SUBLANES, D_MODEL)),
                                 jnp.broadcast_to(wf[:, None, :], (N_HEADS, SUBLANES, D_MODEL))], axis=1).astype(BF)
        o0 = n5 + 2 * N_HEADS
        wu = wl[:, o0:o0 + D_B].astype(BF)
        wz = wl[:, o0 + D_B:o0 + 2 * D_B].astype(BF)
        wga = wl[:, o0 + 2 * D_B:o0 + 2 * D_B + D_MODEL].astype(BF)
        wgb = wl[:, o0 + 2 * D_B + D_MODEL:].astype(BF)
        wmix = w_pool_mix[l].astype(BF)
        ps = pool_scale[l].reshape(1, D_B)
        wa = w_branch_a[l].astype(BF)
        wb = w_branch_b[l].astype(BF)
        wo = w_out[l].astype(BF)
        hnw = head_norm_w[l].reshape(1, D_MODEL)
        last = l == depth - 1
        nw_next = (final_norm_w if last else norm_w[l + 1]).reshape(1, D_MODEL)

        a_p, c1, n1, m1 = _mlstm(hp, w5, wif_c, wif_r, b_if[l], hnw, None,
                                 batch=bp, seq_rows=lp, t=tp, valid=tp)
        bb_p, sga_p, sgb_p, hist_p = _pool(hp, wu, wz, wga, wgb, wmix, ps, None,
                                           batch=bp, seq_rows=lp, t=tp, pos0=0)
        res = _out(xp, a_p, bb_p, sga_p, sgb_p, wa, wb, wo, nw_next, mod_rows,
                   mod_base(l, False), mod_base(l + 1, False), tiles_per_seq=lp // tp, t=tp, last=last)
        if last:
            (yp,) = res
        else:
            xp, hp = res
        outs["cp"].append(c1)
        outs["np"].append(n1[:, :, 0, :])
        outs["mp"].append(m1[:, :, 0, 0])
        outs["pp"].append(hist_p[:, 1:, :])

        hs_pad = jnp.zeros((bs, ts_pad, D_MODEL), BF).at[:, :ls].set(hs.reshape(bs, ls, D_MODEL))
        st = (state_C[l], state_n[l][:, :, None, :],
              jnp.broadcast_to(state_m[l][:, :, None, None], (bs, N_HEADS, 1, LANES)))
        a_s, c2, n2, m2 = _mlstm(hs_pad.reshape(bs * ts_pad, D_MODEL), w5, wif_c, wif_r, b_if[l], hnw, st,
                                 batch=bs, seq_rows=ts_pad, t=ts_pad, valid=ls)
        a_s = a_s.reshape(bs, ts_pad, D_MODEL)[:, :ls].reshape(bs * ls, D_MODEL)
        hist0 = jnp.pad(state_pool[l], ((0, 0), (1, 0), (0, 0)))
        bb_s, sga_s, sgb_s, hist_s = _pool(hs, wu, wz, wga, wgb, wmix, ps, hist0,
                                           batch=bs, seq_rows=ls, t=ls, pos0=PAST_LEN)
        res = _out(xs, a_s, bb_s, sga_s, sgb_s, wa, wb, wo, nw_next, mod_rows,
                   mod_base(l, True), mod_base(l + 1, True), tiles_per_seq=1, t=ls, last=last)
        if last:
            (ys,) = res
        else:
            xs, hs = res
        outs["cs"].append(c2)
        outs["ns"].append(n2[:, :, 0, :])
        outs["ms"].append(m2[:, :, 0, 0])
        outs["ps"].append(hist_s[:, 1:, :])

    st = {k: jnp.stack(v) for k, v in outs.items()}
    return (yp.reshape(bp, lp, D_MODEL), ys.reshape(bs, ls, D_MODEL),
            st["cp"], st["np"], st["mp"], st["pp"],
            st["cs"], st["ns"], st["ms"], st["ps"])
```

```python
import functools

import jax
import jax.numpy as jnp
from jax import lax
from jax.experimental import pallas as pl
from jax.experimental.pallas import tpu as pltpu

D_MODEL = 2048
N_HEADS = 4
DH = D_MODEL // N_HEADS
D_B = D_MODEL // 2
POOL_WINDOWS = (2, 4, 8, 16)
POOL_GROUP = D_B // len(POOL_WINDOWS)
POOL_HIST = 15
HIST_ROWS = 16
PAST_LEN = 2048
EPS = 1e-6
NEG = -1e30
N_QKVOZ = 5 * D_MODEL
N_IF = 2 * N_HEADS
N_REST = 2 * D_B + 2 * D_MODEL

LANES = 128
SUBLANES = 8
VMEM_LIMIT = 60 * 1024 * 1024
MOD_ROWS = 2 * SUBLANES

BF = jnp.bfloat16
F32 = jnp.float32

_dot = functools.partial(jnp.dot, preferred_element_type=F32)


def _dot_nt(a, b):
    return lax.dot_general(a, b, (((1,), (1,)), ((), ())), preferred_element_type=F32)


def _dot_tn(a, b):
    return lax.dot_general(a, b, (((0,), (0,)), ((), ())), preferred_element_type=F32)


def _params(n_axes):
    return pltpu.CompilerParams(dimension_semantics=("arbitrary",) * n_axes,
                                vmem_limit_bytes=VMEM_LIMIT)


def _layer_spec(shape, layer):
    nd = len(shape)
    return pl.BlockSpec((1,) + tuple(shape[1:]), lambda *_: (layer,) + (0,) * (nd - 1),
                        pipeline_mode=pl.Buffered(1))


def _silu(x):
    return x * jax.nn.sigmoid(x)


def _log_sigmoid(x):
    return jnp.minimum(x, 0.0) - jnp.log(1.0 + jnp.exp(-jnp.abs(x)))


def _cast_kernel(w_ref, o_ref):
    o_ref[...] = w_ref[...].astype(BF)


def _cast(w, ncols, rows_blk):
    depth, kdim, _ = w.shape
    spec = pl.BlockSpec((1, rows_blk, ncols), lambda l, i: (l, i, 0))
    return pl.pallas_call(
        _cast_kernel, grid=(depth, kdim // rows_blk), in_specs=[spec], out_specs=spec,
        out_shape=jax.ShapeDtypeStruct((depth, kdim, ncols), BF),
        compiler_params=_params(2), name="cast",
    )(w)


def _shift_cast_kernel(a_ref, b_ref, o_ref):
    o_ref[0] = jnp.concatenate([a_ref[0][:, N_IF:], b_ref[0][:, :N_IF]], axis=1).astype(BF)


def _cast_rest(w_in):
    depth, kdim, _ = w_in.shape
    wide, rows_blk = D_MODEL, 512
    first = N_QKVOZ // wide
    return pl.pallas_call(
        _shift_cast_kernel,
        grid=(depth, N_REST // wide, kdim // rows_blk),
        in_specs=[pl.BlockSpec((1, rows_blk, wide), lambda l, j, i: (l, i, first + j)),
                  pl.BlockSpec((1, rows_blk, LANES), lambda l, j, i: (l, i, (first + j + 1) * (wide // LANES)))],
        out_specs=pl.BlockSpec((1, rows_blk, wide), lambda l, j, i: (l, i, j)),
        out_shape=jax.ShapeDtypeStruct((depth, kdim, N_REST), BF),
        compiler_params=_params(3), name="cast_rest",
    )(w_in, w_in)


def _mod_kernel(c_ref, w_ref, b_ref, o_ref):
    @pl.when(pl.program_id(1) == 0)
    def _init():
        o_ref[0] = jnp.broadcast_to(b_ref[0], o_ref.shape[1:])

    o_ref[0] += _dot(_silu(c_ref[...]).astype(BF), w_ref[0].astype(BF))


def _modulation(c_all, w_ada, b_ada):
    depth, kdim, n = w_ada.shape
    tk = 512
    return pl.pallas_call(
        _mod_kernel,
        grid=(depth, kdim // tk),
        in_specs=[pl.BlockSpec((MOD_ROWS, tk), lambda l, k: (0, k)),
                  pl.BlockSpec((1, tk, n), lambda l, k: (l, k, 0)),
                  pl.BlockSpec((1, 1, n), lambda l, k: (l, 0, 0))],
        out_specs=pl.BlockSpec((1, MOD_ROWS, n), lambda l, k: (l, 0, 0)),
        out_shape=jax.ShapeDtypeStruct((depth, MOD_ROWS, n), F32),
        compiler_params=_params(2), name="modulation",
    )(c_all, w_ada, b_ada.reshape(depth, 1, n))


def _mod_vec(ref, seq, nseq, lseq):
    if nseq == 1:
        return ref[0, 0, pl.ds(seq, 1), :]
    blk = ref[0, 0]
    return jnp.concatenate([jnp.broadcast_to(blk[s:s + 1, :], (lseq, D_MODEL)) for s in range(nseq)], axis=0)


def _mod_spec(layer, kind, group):
    return pl.BlockSpec((1, 1, SUBLANES, D_MODEL), lambda i: (layer, kind, group, 0))


def _norm_mod(y, nw, scl, shift):
    r = lax.rsqrt(jnp.mean(y * y, axis=-1, keepdims=True) + EPS)
    return (y * r * nw) * (1.0 + scl) + shift


def _h_kernel(x_ref, nw_ref, shift_ref, scl_ref, wif_ref, h_ref, g_ref, *, tiles_per_seq, nseq, lseq):
    seq = pl.program_id(0) // tiles_per_seq
    hn = _norm_mod(x_ref[...], nw_ref[0], _mod_vec(scl_ref, seq, nseq, lseq),
                   _mod_vec(shift_ref, seq, nseq, lseq)).astype(BF)
    h_ref[...] = hn
    g_ref[...] = _dot(hn, wif_ref[0])


def _first_h(x2, nw_all, mod_t, wif, *, group, tiles_per_seq, t, nseq, lseq):
    rows = x2.shape[0]
    return pl.pallas_call(
        functools.partial(_h_kernel, tiles_per_seq=tiles_per_seq, nseq=nseq, lseq=lseq),
        grid=(rows // t,),
        in_specs=[pl.BlockSpec((t, D_MODEL), lambda i: (i, 0)),
                  pl.BlockSpec((1, 1, D_MODEL), lambda i: (0, 0, 0)),
                  _mod_spec(0, 0, group), _mod_spec(0, 1, group),
                  pl.BlockSpec((1, D_MODEL, LANES), lambda i: (0, 0, 0))],
        out_specs=[pl.BlockSpec((t, D_MODEL), lambda i: (i, 0)),
                   pl.BlockSpec((t, LANES), lambda i: (i, 0))],
        out_shape=[jax.ShapeDtypeStruct((rows, D_MODEL), BF),
                   jax.ShapeDtypeStruct((rows, LANES), F32)],
        compiler_params=_params(1), name="first_h",
    )(x2, nw_all, mod_t, mod_t, wif)


def _gate_columns(g, hd, b_i, b_f):
    lane = lax.broadcasted_iota(jnp.int32, g.shape, 1)
    ig = jnp.sum(jnp.where(lane == hd, g, 0.0), axis=1, keepdims=True) + b_i
    fg = jnp.sum(jnp.where(lane == hd + N_HEADS, g, 0.0), axis=1, keepdims=True) + b_f
    return ig, _log_sigmoid(fg)


def _recurrence(q, k, v, ig_c, lf_c, c_old, n_old, m0):
    t = q.shape[0]
    qb = q.astype(BF)
    vb = v.astype(BF)
    row = lax.broadcasted_iota(jnp.int32, (t, t), 0)
    col = lax.broadcasted_iota(jnp.int32, (t, t), 1)
    causal = col <= row
    eye = col == row
    ig_r = jnp.sum(jnp.where(eye, ig_c, 0.0), axis=0, keepdims=True)
    lf_r = jnp.sum(jnp.where(eye, lf_c, 0.0), axis=0, keepdims=True)
    b_c = jnp.sum(jnp.where(causal, lf_r, 0.0), axis=1, keepdims=True)
    b_r = jnp.sum(jnp.where(row <= col, lf_c, 0.0), axis=0, keepdims=True)
    g_r = ig_r - b_r
    g_c = ig_c - b_c
    big_m = jnp.maximum(jnp.max(jnp.where(causal, g_r, NEG), axis=1, keepdims=True), m0)
    w_intra = jnp.exp(jnp.where(causal, g_r - big_m, NEG))
    a_inter = jnp.exp(m0 - big_m)

    s = _dot_nt(qb, k.astype(BF)) * w_intra
    num = a_inter * _dot(qb, c_old.astype(BF)) + _dot(s.astype(BF), vb)
    den = a_inter * jnp.sum(q * n_old, axis=1, keepdims=True) + jnp.sum(s, axis=1, keepdims=True)
    h = num * (1.0 / jnp.maximum(jnp.abs(den), jnp.exp(-(b_c + big_m))))

    m_last = big_m[t - 1:t, :]
    kw = k * jnp.exp(g_c - m_last)
    decay = jnp.exp(m0 - m_last)
    c_new = decay * c_old + _dot_tn(kw.astype(BF), vb)
    n_new = decay * n_old + jnp.sum(kw, axis=0, keepdims=True)
    return h, c_new, n_new, b_c[t - 1:t, :] + m_last


def _head_out(o, z, h, hnw):
    og = jax.nn.sigmoid(o) * h
    mu = jnp.mean(og, axis=1, keepdims=True)
    dev = og - mu
    var = jnp.mean(dev * dev, axis=1, keepdims=True)
    return (_silu(z) * (dev * lax.rsqrt(var + EPS) * hnw)).astype(BF)


def _mlstm_kernel(bif_ref, h_ref, g_ref, wq_ref, wk_ref, wv_ref, wo_ref, wz_ref, hnw_ref,
                  a_ref, c_ref, n_ref, m_ref, *, layer):
    hd = pl.program_id(0)

    @pl.when(pl.program_id(2) == 0)
    def _init():
        c_ref[0, 0] = jnp.zeros((DH, DH), F32)
        n_ref[0, 0] = jnp.zeros((1, DH), F32)
        m_ref[0, 0] = jnp.zeros((1, LANES), F32)

    hb = h_ref[...]
    q = _dot(hb, wq_ref[0])
    k = _dot(hb, wk_ref[0]) * (DH ** -0.5)
    v = _dot(hb, wv_ref[0])
    ig_c, lf_c = _gate_columns(g_ref[...], hd, bif_ref[layer, hd], bif_ref[layer, N_HEADS + hd])
    h, c_new, n_new, m_new = _recurrence(q, k, v, ig_c, lf_c, c_ref[0, 0], n_ref[0, 0], m_ref[0, 0][:, 0:1])
    c_ref[0, 0] = c_new
    n_ref[0, 0] = n_new
    m_ref[0, 0] = jnp.broadcast_to(m_new, (1, LANES))
    a_ref[...] = _head_out(_dot(hb, wo_ref[0]), _dot(hb, wz_ref[0]), h, hnw_ref[0])


def _w5_specs(layer, index_map):
    return [pl.BlockSpec((1, D_MODEL, DH), functools.partial(index_map, layer, g)) for g in range(5)]


def _mlstm_prompt(h2, gates, w5, b_if, hnw, *, layer, batch, seq_rows, t):
    nck = seq_rows // t
    rows = batch * seq_rows
    state_specs = [pl.BlockSpec((1, 1, DH, DH), lambda hd, b, c: (b, hd, 0, 0)),
                   pl.BlockSpec((1, 1, 1, DH), lambda hd, b, c: (b, hd, 0, 0)),
                   pl.BlockSpec((1, 1, 1, LANES), lambda hd, b, c: (b, hd, 0, 0))]
    return pl.pallas_call(
        functools.partial(_mlstm_kernel, layer=layer),
        grid=(N_HEADS, batch, nck),
        in_specs=[pl.BlockSpec(memory_space=pltpu.SMEM),
                  pl.BlockSpec((t, D_MODEL), lambda hd, b, c: (b * nck + c, 0)),
                  pl.BlockSpec((t, LANES), lambda hd, b, c: (b * nck + c, 0))]
                 + _w5_specs(layer, lambda l, g, hd, b, c: (l, 0, g * N_HEADS + hd))
                 + [pl.BlockSpec((1, 1, DH), lambda hd, b, c: (layer, 0, hd))],
        out_specs=[pl.BlockSpec((t, DH), lambda hd, b, c: (b * nck + c, hd))] + state_specs,
        out_shape=[jax.ShapeDtypeStruct((rows, D_MODEL), BF),
                   jax.ShapeDtypeStruct((batch, N_HEADS, DH, DH), F32),
                   jax.ShapeDtypeStruct((batch, N_HEADS, 1, DH), F32),
                   jax.ShapeDtypeStruct((batch, N_HEADS, 1, LANES), F32)],
        compiler_params=_params(3), name="mlstm",
    )(b_if, h2, gates, w5, w5, w5, w5, w5, hnw)


def _mlstm_seqs_kernel(bif_ref, h_ref, g_ref, wq_ref, wk_ref, wv_ref, wo_ref, wz_ref, hnw_ref,
                       c0_ref, n0_ref, m0_ref, a_ref, c_ref, n_ref, m_ref,
                       q_s, k_s, v_s, h_s, ig_s, lf_s, *, layer, lseq):
    hd = pl.program_id(0)
    b = pl.program_id(1)

    @pl.when(b == 0)
    def _project():
        hb = h_ref[...]
        q_s[...] = _dot(hb, wq_ref[0])
        k_s[...] = _dot(hb, wk_ref[0]) * (DH ** -0.5)
        v_s[...] = _dot(hb, wv_ref[0])
        ig_c, lf_c = _gate_columns(g_ref[...], hd, bif_ref[layer, hd], bif_ref[layer, N_HEADS + hd])
        ig_s[...] = ig_c
        lf_s[...] = lf_c

    rows = pl.ds(pl.multiple_of(b * lseq, lseq), lseq)
    h, c_new, n_new, m_new = _recurrence(q_s[rows, :], k_s[rows, :], v_s[rows, :], ig_s[rows, :], lf_s[rows, :],
                                         c0_ref[0, 0, 0], n0_ref[0, 0, 0], m0_ref[0, 0, 0][:, 0:1])
    c_ref[0, 0] = c_new
    n_ref[0, 0] = n_new
    m_ref[0, 0] = jnp.broadcast_to(m_new, (1, LANES))
    h_s[rows, :] = h

    @pl.when(b == pl.num_programs(1) - 1)
    def _finish():
        hb = h_ref[...]
        a_ref[...] = _head_out(_dot(hb, wo_ref[0]), _dot(hb, wz_ref[0]), h_s[...], hnw_ref[0])


def _mlstm_seqs(h2, gates, w5, b_if, hnw, state_c, state_n, state_m, *, layer, batch, lseq):
    rows = batch * lseq
    out_state_specs = [pl.BlockSpec((1, 1, DH, DH), lambda hd, b: (b, hd, 0, 0)),
                       pl.BlockSpec((1, 1, 1, DH), lambda hd, b: (b, hd, 0, 0)),
                       pl.BlockSpec((1, 1, 1, LANES), lambda hd, b: (b, hd, 0, 0))]
    in_state_specs = [pl.BlockSpec((1, 1, 1, DH, DH), lambda hd, b: (layer, b, hd, 0, 0)),
                      pl.BlockSpec((1, 1, 1, 1, DH), lambda hd, b: (layer, b, hd, 0, 0)),
                      pl.BlockSpec((1, 1, 1, 1, LANES), lambda hd, b: (layer, b, hd, 0, 0))]
    return pl.pallas_call(
        functools.partial(_mlstm_seqs_kernel, layer=layer, lseq=lseq),
        grid=(N_HEADS, batch),
        in_specs=[pl.BlockSpec(memory_space=pltpu.SMEM),
                  pl.BlockSpec((rows, D_MODEL), lambda hd, b: (0, 0)),
                  pl.BlockSpec((rows, LANES), lambda hd, b: (0, 0))]
                 + _w5_specs(layer, lambda l, g, hd, b: (l, 0, g * N_HEADS + hd))
                 + [pl.BlockSpec((1, 1, DH), lambda hd, b: (layer, 0, hd))]
                 + in_state_specs,
        out_specs=[pl.BlockSpec((rows, DH), lambda hd, b: (0, hd))] + out_state_specs,
        out_shape=[jax.ShapeDtypeStruct((rows, D_MODEL), BF),
                   jax.ShapeDtypeStruct((batch, N_HEADS, DH, DH), F32),
                   jax.ShapeDtypeStruct((batch, N_HEADS, 1, DH), F32),
                   jax.ShapeDtypeStruct((batch, N_HEADS, 1, LANES), F32)],
        scratch_shapes=[pltpu.VMEM((rows, DH), F32)] * 4 + [pltpu.VMEM((rows, 1), F32)] * 2,
        compiler_params=_params(2), name="mlstm_seqs",
    )(b_if, h2, gates, w5, w5, w5, w5, w5, hnw, state_c, state_n, state_m)


def _pool_kernel(*refs, lseq, nseq, pos0, has_state):
    if has_state:
        h_ref, w_ref, wmix_ref, ps_ref, hist0_ref, bb_ref, sga_ref, sgb_ref, hist_ref, uext = refs
    else:
        h_ref, w_ref, wmix_ref, ps_ref, bb_ref, sga_ref, sgb_ref, hist_ref, uext = refs
    i = pl.program_id(1)
    seg = HIST_ROWS + lseq

    @pl.when(i == 0)
    def _first():
        for sq in range(nseq):
            uext[sq * seg:sq * seg + HIST_ROWS, :] = (hist0_ref[0, sq] if has_state
                                                       else jnp.zeros((HIST_ROWS, D_B), F32))

    @pl.when(i > 0)
    def _carry():
        uext[0:HIST_ROWS, :] = uext[lseq:seg, :]

    hb = h_ref[...]
    u = _dot(hb, w_ref[0, :, 0:D_B])
    for sq in range(nseq):
        uext[sq * seg + HIST_ROWS:(sq + 1) * seg, :] = u[sq * lseq:(sq + 1) * lseq, :]
    zb = _dot(hb, w_ref[0, :, D_B:2 * D_B])
    sga_ref[...] = jax.nn.sigmoid(_dot(hb, w_ref[0, :, 2 * D_B:2 * D_B + D_MODEL]))
    sgb_ref[...] = jax.nn.sigmoid(_dot(hb, w_ref[0, :, 2 * D_B + D_MODEL:]))

    in_seq = lax.broadcasted_iota(jnp.int32, (lseq, 1), 0)
    pos = jnp.concatenate([in_seq] * nseq, axis=0) + (i * lseq + pos0 + 1)
    for g, w in enumerate(POOL_WINDOWS):
        sl = slice(g * POOL_GROUP, (g + 1) * POOL_GROUP)
        acc = uext[:, sl]
        shift = 1
        while shift < w:
            acc = acc + pltpu.roll(acc, shift, axis=0)
            shift *= 2
        win = jnp.concatenate([acc[sq * seg + HIST_ROWS:(sq + 1) * seg, :] for sq in range(nseq)], axis=0)
        pooled = win / jnp.minimum(pos, w).astype(F32)
        dlt = (pooled - u[:, sl]).astype(BF)
        mixed = _dot(dlt, wmix_ref[0, g * POOL_GROUP:(g + 1) * POOL_GROUP, :]) * ps_ref[0][:, sl]
        bb_ref[:, sl] = (_silu(zb[:, sl]) * mixed).astype(BF)

    @pl.when(i == pl.num_programs(1) - 1)
    def _hist():
        for sq in range(nseq):
            hist_ref[sq] = uext[sq * seg + lseq:(sq + 1) * seg, :]


def _pool(h2, w_rest, wmix, ps, hist0, *, layer, groups, tiles, lseq, nseq, pos0):
    t = lseq * nseq
    rows = groups * tiles * t
    has_state = hist0 is not None
    in_specs = [pl.BlockSpec((t, D_MODEL), lambda b, i: (b * tiles + i, 0)),
                _layer_spec(w_rest.shape, layer), _layer_spec(wmix.shape, layer), _layer_spec(ps.shape, layer)]
    args = [h2, w_rest, wmix, ps]
    if has_state:
        in_specs.append(pl.BlockSpec((1, nseq, HIST_ROWS, D_B), lambda b, i: (layer, b, 0, 0)))
        args.append(hist0)
    row_spec = lambda width: pl.BlockSpec((t, width), lambda b, i: (b * tiles + i, 0))
    return pl.pallas_call(
        functools.partial(_pool_kernel, lseq=lseq, nseq=nseq, pos0=pos0, has_state=has_state),
        grid=(groups, tiles),
        in_specs=in_specs,
        out_specs=[row_spec(D_B), row_spec(D_MODEL), row_spec(D_MODEL),
                   pl.BlockSpec((nseq, HIST_ROWS, D_B), lambda b, i: (b, 0, 0))],
        out_shape=[jax.ShapeDtypeStruct((rows, D_B), BF),
                   jax.ShapeDtypeStruct((rows, D_MODEL), F32),
                   jax.ShapeDtypeStruct((rows, D_MODEL), F32),
                   jax.ShapeDtypeStruct((groups * nseq, HIST_ROWS, D_B), F32)],
        scratch_shapes=[pltpu.VMEM((nseq * (HIST_ROWS + lseq), D_B), F32)],
        compiler_params=_params(2), name="pool",
    )(*args)


def _out_kernel(*refs, last, tiles_per_seq, nseq, lseq):
    if last:
        x_ref, a_ref, bb_ref, sga_ref, sgb_ref, gate_ref, wa_ref, wb_ref, wo_ref, nw_ref, y_ref = refs
    else:
        (x_ref, a_ref, bb_ref, sga_ref, sgb_ref, gate_ref, wa_ref, wb_ref, wo_ref, nw_ref,
         shift_ref, scl_ref, wif_ref, y_ref, hn_ref, g_ref) = refs
    seq = pl.program_id(0) // tiles_per_seq
    br_a = _dot(a_ref[...], wa_ref[0])
    br_b = _dot(bb_ref[...], wb_ref[0])
    merged = (sga_ref[...] * br_a + sgb_ref[...] * br_b).astype(BF)
    y = x_ref[...] + _mod_vec(gate_ref, seq, nseq, lseq) * _dot(merged, wo_ref[0])
    if last:
        r = lax.rsqrt(jnp.mean(y * y, axis=-1, keepdims=True) + EPS)
        y_ref[...] = y * r * nw_ref[0]
    else:
        y_ref[...] = y
        hn = _norm_mod(y, nw_ref[0], _mod_vec(scl_ref, seq, nseq, lseq),
                       _mod_vec(shift_ref, seq, nseq, lseq)).astype(BF)
        hn_ref[...] = hn
        g_ref[...] = _dot(hn, wif_ref[0])


def _out(x2, a2, bb2, sga, sgb, mod_t, wa, wb, wo, nw_all, wif, *, layer, group, tiles_per_seq, t, nseq, lseq,
         last):
    rows = x2.shape[0]
    row_spec = lambda width: pl.BlockSpec((t, width), lambda i: (i, 0))
    in_specs = [row_spec(D_MODEL), row_spec(D_MODEL), row_spec(D_B), row_spec(D_MODEL), row_spec(D_MODEL),
                _mod_spec(layer, 2, group),
                _layer_spec(wa.shape, layer), _layer_spec(wb.shape, layer), _layer_spec(wo.shape, layer),
                pl.BlockSpec((1, 1, D_MODEL), lambda i: (layer + 1, 0, 0))]
    args = [x2, a2, bb2, sga, sgb, mod_t, wa, wb, wo, nw_all]
    if last:
        out_specs = [row_spec(D_MODEL)]
        out_shape = [jax.ShapeDtypeStruct((rows, D_MODEL), F32)]
    else:
        in_specs += [_mod_spec(layer + 1, 0, group), _mod_spec(layer + 1, 1, group),
                     pl.BlockSpec((1, D_MODEL, LANES), lambda i: (layer + 1, 0, 0))]
        args += [mod_t, mod_t, wif]
        out_specs = [row_spec(D_MODEL), row_spec(D_MODEL), row_spec(LANES)]
        out_shape = [jax.ShapeDtypeStruct((rows, D_MODEL), F32),
                     jax.ShapeDtypeStruct((rows, D_MODEL), BF),
                     jax.ShapeDtypeStruct((rows, LANES), F32)]
    return pl.pallas_call(
        functools.partial(_out_kernel, last=last, tiles_per_seq=tiles_per_seq, nseq=nseq, lseq=lseq),
        grid=(rows // t,),
        in_specs=in_specs, out_specs=out_specs, out_shape=out_shape,
        compiler_params=_params(1), name="out",
    )(*args)


def kernel(x_prompt, x_sample, c_prompt, c_sample, state_C, state_n, state_m, state_pool,
           norm_w, w_ada, b_ada, w_in, b_if, head_norm_w, w_pool_mix, pool_scale,
           w_branch_a, w_branch_b, w_out, final_norm_w):
    depth = w_in.shape[0]
    bp, lp, _ = x_prompt.shape
    bs, ls, _ = x_sample.shape
    tp = 256
    assert bp <= SUBLANES and bs == SUBLANES and lp % tp == 0 and ls >= POOL_HIST and ls % SUBLANES == 0

    w5 = _cast(w_in, N_QKVOZ, 128)
    w_rest = _cast_rest(w_in)
    wa = _cast(w_branch_a, D_MODEL, 512)
    wb = _cast(w_branch_b, D_MODEL, 512)
    wo = _cast(w_out, D_MODEL, 512)
    wmix = _cast(w_pool_mix.reshape(depth, D_B, POOL_GROUP), POOL_GROUP, D_B)
    wif = jnp.pad(w_in[:, :, N_QKVOZ:N_QKVOZ + N_IF], ((0, 0), (0, 0), (0, LANES - N_IF))).astype(BF)

    c_all = jnp.zeros((MOD_ROWS, D_MODEL), F32).at[:bp].set(c_prompt).at[SUBLANES:].set(c_sample)
    mod = _modulation(c_all, w_ada, b_ada)
    mod_t = mod.reshape(depth, MOD_ROWS, 3, D_MODEL).transpose(0, 2, 1, 3)

    nw_all = jnp.concatenate([norm_w, final_norm_w[None, :]], axis=0).reshape(depth + 1, 1, D_MODEL)
    hnw = head_norm_w.reshape(depth, 1, D_MODEL)
    ps = pool_scale.reshape(depth, 1, D_B)
    st_n = state_n.reshape(depth, bs, N_HEADS, 1, DH)
    st_m = jnp.broadcast_to(state_m[..., None, None], (depth, bs, N_HEADS, 1, LANES))
    hist0 = jnp.pad(state_pool, ((0, 0), (0, 0), (HIST_ROWS - POOL_HIST, 0), (0, 0)))

    xp = x_prompt.reshape(bp * lp, D_MODEL)
    xs = x_sample.reshape(bs * ls, D_MODEL)
    prompt = dict(group=0, tiles_per_seq=lp // tp, t=tp, nseq=1, lseq=tp)
    sample = dict(group=1, tiles_per_seq=1, t=bs * ls, nseq=bs, lseq=ls)
    hp, gp = _first_h(xp, nw_all, mod_t, wif, **prompt)
    hs, gs = _first_h(xs, nw_all, mod_t, wif, **sample)

    outs = {k: [] for k in ("cp", "np", "mp", "pp", "cs", "ns", "ms", "ps")}
    for l in range(depth):
        last = l == depth - 1
        a_p, c1, n1, m1 = _mlstm_prompt(hp, gp, w5, b_if, hnw, layer=l, batch=bp, seq_rows=lp, t=tp)
        bb_p, sga_p, sgb_p, hist_p = _pool(hp, w_rest, wmix, ps, None, layer=l, groups=bp, tiles=lp // tp,
                                           lseq=tp, nseq=1, pos0=0)
        res = _out(xp, a_p, bb_p, sga_p, sgb_p, mod_t, wa, wb, wo, nw_all, wif, layer=l, last=last, **prompt)
        if last:
            (yp,) = res
        else:
            xp, hp, gp = res
        outs["cp"].append(c1)
        outs["np"].append(n1[:, :, 0, :])
        outs["mp"].append(m1[:, :, 0, 0])
        outs["pp"].append(hist_p[:, HIST_ROWS - POOL_HIST:, :])

        a_s, c2, n2, m2 = _mlstm_seqs(hs, gs, w5, b_if, hnw, state_C, st_n, st_m, layer=l, batch=bs, lseq=ls)
        bb_s, sga_s, sgb_s, hist_s = _pool(hs, w_rest, wmix, ps, hist0, layer=l, groups=1, tiles=1,
                                           lseq=ls, nseq=bs, pos0=PAST_LEN)
        res = _out(xs, a_s, bb_s, sga_s, sgb_s, mod_t, wa, wb, wo, nw_all, wif, layer=l, last=last, **sample)
        if last:
            (ys,) = res
        else:
            xs, hs, gs = res
        outs["cs"].append(c2)
        outs["ns"].append(n2[:, :, 0, :])
        outs["ms"].append(m2[:, :, 0, 0])
        outs["ps"].append(hist_s[:, HIST_ROWS - POOL_HIST:, :])

    st = {k: jnp.stack(v) for k, v in outs.items()}
    return (yp.reshape(bp, lp, D_MODEL), ys.reshape(bs, ls, D_MODEL),
            st["cp"], st["np"], st["mp"], st["pp"],
            st["cs"], st["ns"], st["ms"], st["ps"])
```

```python
import functools

import jax
import jax.numpy as jnp
from jax import lax
from jax.experimental import pallas as pl
from jax.experimental.pallas import tpu as pltpu

D_MODEL = 2048
N_HEADS = 4
DH = D_MODEL // N_HEADS
D_B = D_MODEL // 2
POOL_WINDOWS = (2, 4, 8, 16)
POOL_GROUP = D_B // len(POOL_WINDOWS)
POOL_HIST = 15
HIST_ROWS = 16
PAST_LEN = 2048
EPS = 1e-6
NEG = -1e30
N_QKVOZ = 5 * D_MODEL
N_IF = 2 * N_HEADS
N_REST = 2 * D_B + 2 * D_MODEL

LANES = 128
SUBLANES = 8
VMEM_LIMIT = 60 * 1024 * 1024
MOD_ROWS = 2 * SUBLANES

BF = jnp.bfloat16
F32 = jnp.float32

_dot = functools.partial(jnp.dot, preferred_element_type=F32)


def _dot_nt(a, b):
    return lax.dot_general(a, b, (((1,), (1,)), ((), ())), preferred_element_type=F32)


def _dot_tn(a, b):
    return lax.dot_general(a, b, (((0,), (0,)), ((), ())), preferred_element_type=F32)


def _params(n_axes):
    return pltpu.CompilerParams(dimension_semantics=("arbitrary",) * n_axes,
                                vmem_limit_bytes=VMEM_LIMIT)


def _layer_spec(shape, layer):
    nd = len(shape)
    return pl.BlockSpec((1,) + tuple(shape[1:]), lambda *_: (layer,) + (0,) * (nd - 1),
                        pipeline_mode=pl.Buffered(1))


def _silu(x):
    return x * jax.nn.sigmoid(x)


def _log_sigmoid(x):
    return jnp.minimum(x, 0.0) - jnp.log(1.0 + jnp.exp(-jnp.abs(x)))


def _cast_kernel(w_ref, o_ref):
    o_ref[...] = w_ref[...].astype(BF)


def _cast(w, ncols, rows_blk):
    depth, kdim, _ = w.shape
    spec = pl.BlockSpec((1, rows_blk, ncols), lambda l, i: (l, i, 0))
    return pl.pallas_call(
        _cast_kernel, grid=(depth, kdim // rows_blk), in_specs=[spec], out_specs=spec,
        out_shape=jax.ShapeDtypeStruct((depth, kdim, ncols), BF),
        compiler_params=_params(2), name="cast",
    )(w)


def _tcast_kernel(w_ref, o_ref):
    o_ref[0] = w_ref[0].T.astype(BF)


def _tcast_shift_kernel(a_ref, b_ref, o_ref):
    rows = jnp.concatenate([a_ref[0][N_IF:, :], b_ref[0]], axis=0)
    o_ref[0] = rows.T.astype(BF)


def _tcast_gate_kernel(w_ref, o_ref):
    rows = jnp.concatenate([w_ref[0], jnp.zeros((LANES - N_IF, D_MODEL), F32)], axis=0)
    o_ref[0] = rows.T.astype(BF)


def _cast_w_in(w_in_t):
    depth, _, kdim = w_in_t.shape
    nb = 1024
    first = N_QKVOZ // nb
    w5 = pl.pallas_call(
        _tcast_kernel, grid=(depth, N_QKVOZ // nb),
        in_specs=[pl.BlockSpec((1, nb, kdim), lambda l, j: (l, j, 0))],
        out_specs=pl.BlockSpec((1, kdim, nb), lambda l, j: (l, 0, j)),
        out_shape=jax.ShapeDtypeStruct((depth, kdim, N_QKVOZ), BF),
        compiler_params=_params(2), name="cast_w5",
    )(w_in_t)
    w_rest = pl.pallas_call(
        _tcast_shift_kernel, grid=(depth, N_REST // nb),
        in_specs=[pl.BlockSpec((1, nb, kdim), lambda l, j: (l, first + j, 0)),
                  pl.BlockSpec((1, N_IF, kdim), lambda l, j: (l, (first + j + 1) * (nb // N_IF), 0))],
        out_specs=pl.BlockSpec((1, kdim, nb), lambda l, j: (l, 0, j)),
        out_shape=jax.ShapeDtypeStruct((depth, kdim, N_REST), BF),
        compiler_params=_params(2), name="cast_rest",
    )(w_in_t, w_in_t)
    wif = pl.pallas_call(
        _tcast_gate_kernel, grid=(depth,),
        in_specs=[pl.BlockSpec((1, N_IF, kdim), lambda l: (l, N_QKVOZ // N_IF, 0))],
        out_specs=pl.BlockSpec((1, kdim, LANES), lambda l: (l, 0, 0)),
        out_shape=jax.ShapeDtypeStruct((depth, kdim, LANES), BF),
        compiler_params=_params(1), name="cast_gates",
    )(w_in_t)
    return w5, w_rest, wif


def _mod_kernel(c_ref, w_ref, b_ref, o_ref):
    @pl.when(pl.program_id(1) == 0)
    def _init():
        o_ref[0] = jnp.broadcast_to(b_ref[0], o_ref.shape[1:])

    o_ref[0] += _dot(_silu(c_ref[...]).astype(BF), w_ref[0].astype(BF))


def _modulation(c_all, w_ada, b_ada):
    depth, kdim, n = w_ada.shape
    tk = 512
    return pl.pallas_call(
        _mod_kernel,
        grid=(depth, kdim // tk),
        in_specs=[pl.BlockSpec((MOD_ROWS, tk), lambda l, k: (0, k)),
                  pl.BlockSpec((1, tk, n), lambda l, k: (l, k, 0)),
                  pl.BlockSpec((1, 1, n), lambda l, k: (l, 0, 0))],
        out_specs=pl.BlockSpec((1, MOD_ROWS, n), lambda l, k: (l, 0, 0)),
        out_shape=jax.ShapeDtypeStruct((depth, MOD_ROWS, n), F32),
        compiler_params=_params(2), name="modulation",
    )(c_all, w_ada, b_ada.reshape(depth, 1, n))


def _mod_vec(ref, seq, nseq, lseq):
    if nseq == 1:
        return ref[0, 0, pl.ds(seq, 1), :]
    blk = ref[0, 0]
    return jnp.concatenate([jnp.broadcast_to(blk[s:s + 1, :], (lseq, D_MODEL)) for s in range(nseq)], axis=0)


def _mod_spec(layer, kind, group):
    return pl.BlockSpec((1, 1, SUBLANES, D_MODEL), lambda i: (layer, kind, group, 0))


def _norm_mod(y, nw, scl, shift):
    r = lax.rsqrt(jnp.mean(y * y, axis=-1, keepdims=True) + EPS)
    return (y * r * nw) * (1.0 + scl) + shift


def _h_kernel(x_ref, nw_ref, shift_ref, scl_ref, wif_ref, h_ref, g_ref, *, tiles_per_seq, nseq, lseq):
    seq = pl.program_id(0) // tiles_per_seq
    hn = _norm_mod(x_ref[...], nw_ref[0], _mod_vec(scl_ref, seq, nseq, lseq),
                   _mod_vec(shift_ref, seq, nseq, lseq)).astype(BF)
    h_ref[...] = hn
    g_ref[...] = _dot(hn, wif_ref[0])


def _first_h(x2, nw_all, mod_t, wif, *, group, tiles_per_seq, t, nseq, lseq):
    rows = x2.shape[0]
    return pl.pallas_call(
        functools.partial(_h_kernel, tiles_per_seq=tiles_per_seq, nseq=nseq, lseq=lseq),
        grid=(rows // t,),
        in_specs=[pl.BlockSpec((t, D_MODEL), lambda i: (i, 0)),
                  pl.BlockSpec((1, 1, D_MODEL), lambda i: (0, 0, 0)),
                  _mod_spec(0, 0, group), _mod_spec(0, 1, group),
                  pl.BlockSpec((1, D_MODEL, LANES), lambda i: (0, 0, 0))],
        out_specs=[pl.BlockSpec((t, D_MODEL), lambda i: (i, 0)),
                   pl.BlockSpec((t, LANES), lambda i: (i, 0))],
        out_shape=[jax.ShapeDtypeStruct((rows, D_MODEL), BF),
                   jax.ShapeDtypeStruct((rows, LANES), F32)],
        compiler_params=_params(1), name="first_h",
    )(x2, nw_all, mod_t, mod_t, wif)


def _gate_columns(g, hd, b_i, b_f):
    lane = lax.broadcasted_iota(jnp.int32, g.shape, 1)
    ig = jnp.sum(jnp.where(lane == hd, g, 0.0), axis=1, keepdims=True) + b_i
    fg = jnp.sum(jnp.where(lane == hd + N_HEADS, g, 0.0), axis=1, keepdims=True) + b_f
    return ig, _log_sigmoid(fg)


def _recurrence(q, k, v, ig_c, lf_c, c_old, n_old, m0):
    t = q.shape[0]
    qb = q.astype(BF)
    vb = v.astype(BF)
    row = lax.broadcasted_iota(jnp.int32, (t, t), 0)
    col = lax.broadcasted_iota(jnp.int32, (t, t), 1)
    causal = col <= row
    eye = col == row
    ig_r = jnp.sum(jnp.where(eye, ig_c, 0.0), axis=0, keepdims=True)
    lf_r = jnp.sum(jnp.where(eye, lf_c, 0.0), axis=0, keepdims=True)
    b_c = jnp.sum(jnp.where(causal, lf_r, 0.0), axis=1, keepdims=True)
    b_r = jnp.sum(jnp.where(row <= col, lf_c, 0.0), axis=0, keepdims=True)
    g_r = ig_r - b_r
    g_c = ig_c - b_c
    big_m = jnp.maximum(jnp.max(jnp.where(causal, g_r, NEG), axis=1, keepdims=True), m0)
    w_intra = jnp.exp(jnp.where(causal, g_r - big_m, NEG))
    a_inter = jnp.exp(m0 - big_m)

    s = _dot_nt(qb, k.astype(BF)) * w_intra
    num = a_inter * _dot(qb, c_old.astype(BF)) + _dot(s.astype(BF), vb)
    den = a_inter * jnp.sum(q * n_old, axis=1, keepdims=True) + jnp.sum(s, axis=1, keepdims=True)
    h = num * (1.0 / jnp.maximum(jnp.abs(den), jnp.exp(-(b_c + big_m))))

    m_last = big_m[t - 1:t, :]
    kw = k * jnp.exp(g_c - m_last)
    decay = jnp.exp(m0 - m_last)
    c_new = decay * c_old + _dot_tn(kw.astype(BF), vb)
    n_new = decay * n_old + jnp.sum(kw, axis=0, keepdims=True)
    return h, c_new, n_new, b_c[t - 1:t, :] + m_last


def _head_out(o, z, h, hnw):
    og = jax.nn.sigmoid(o) * h
    mu = jnp.mean(og, axis=1, keepdims=True)
    dev = og - mu
    var = jnp.mean(dev * dev, axis=1, keepdims=True)
    return (_silu(z) * (dev * lax.rsqrt(var + EPS) * hnw)).astype(BF)


def _mlstm_kernel(bif_ref, h_ref, g_ref, wq_ref, wk_ref, wv_ref, wo_ref, wz_ref, hnw_ref, *rest,
                  layer, chunk):
    a_ref, c_ref, n_ref, m_ref = rest[-4:]
    hd = pl.program_id(0)

    @pl.when(pl.program_id(2) == 0)
    def _init():
        c_ref[0, 0, 0] = jnp.zeros((DH, DH), F32)
        n_ref[0, 0] = jnp.zeros((1, DH), F32)
        m_ref[0, 0] = jnp.zeros((1, LANES), F32)

    hb = h_ref[...]
    q = _dot(hb, wq_ref[0])
    k = _dot(hb, wk_ref[0]) * (DH ** -0.5)
    v = _dot(hb, wv_ref[0])
    ig_c, lf_c = _gate_columns(g_ref[...], hd, bif_ref[layer, hd], bif_ref[layer, N_HEADS + hd])
    c, n, m = c_ref[0, 0, 0], n_ref[0, 0], m_ref[0, 0][:, 0:1]
    hs = []
    for j in range(hb.shape[0] // chunk):
        r = slice(j * chunk, (j + 1) * chunk)
        h, c, n, m = _recurrence(q[r], k[r], v[r], ig_c[r], lf_c[r], c, n, m)
        hs.append(h)
    c_ref[0, 0, 0] = c
    n_ref[0, 0] = n
    m_ref[0, 0] = jnp.broadcast_to(m, (1, LANES))
    a_ref[...] = _head_out(_dot(hb, wo_ref[0]), _dot(hb, wz_ref[0]), jnp.concatenate(hs, axis=0), hnw_ref[0])


def _w5_specs(layer, index_map):
    return [pl.BlockSpec((1, D_MODEL, DH), functools.partial(index_map, layer, g)) for g in range(5)]


def _state_out(depth, batch, layer, c_stack, index_map, n_inputs):
    specs = [pl.BlockSpec((1, 1, 1, DH, DH), lambda *g: (layer,) + index_map(*g)),
             pl.BlockSpec((1, 1, 1, DH), index_map),
             pl.BlockSpec((1, 1, 1, LANES), index_map)]
    shapes = [jax.ShapeDtypeStruct((depth, batch, N_HEADS, DH, DH), F32),
              jax.ShapeDtypeStruct((batch, N_HEADS, 1, DH), F32),
              jax.ShapeDtypeStruct((batch, N_HEADS, 1, LANES), F32)]
    if c_stack is None:
        return specs, shapes, [], [], {}
    return specs, shapes, [pl.BlockSpec(memory_space=pl.ANY)], [c_stack], {n_inputs: 1}


def _mlstm_prompt(h2, gates, w5, b_if, hnw, c_stack, *, depth, layer, batch, seq_rows, t, chunk):
    nck = seq_rows // t
    rows = batch * seq_rows
    in_specs = ([pl.BlockSpec(memory_space=pltpu.SMEM),
                 pl.BlockSpec((t, D_MODEL), lambda hd, b, c: (b * nck + c, 0)),
                 pl.BlockSpec((t, LANES), lambda hd, b, c: (b * nck + c, 0))]
                + _w5_specs(layer, lambda l, g, hd, b, c: (l, 0, g * N_HEADS + hd))
                + [pl.BlockSpec((1, 1, DH), lambda hd, b, c: (layer, 0, hd))])
    st_specs, st_shapes, extra_specs, extra_args, aliases = _state_out(
        depth, batch, layer, c_stack, lambda hd, b, c: (b, hd, 0, 0), len(in_specs))
    return pl.pallas_call(
        functools.partial(_mlstm_kernel, layer=layer, chunk=chunk),
        grid=(N_HEADS, batch, nck),
        in_specs=in_specs + extra_specs,
        out_specs=[pl.BlockSpec((t, DH), lambda hd, b, c: (b * nck + c, hd))] + st_specs,
        out_shape=[jax.ShapeDtypeStruct((rows, D_MODEL), BF)] + st_shapes,
        input_output_aliases=aliases,
        compiler_params=_params(3), name="mlstm",
    )(b_if, h2, gates, w5, w5, w5, w5, w5, hnw, *extra_args)


def _mlstm_seqs_kernel(bif_ref, h_ref, g_ref, wq_ref, wk_ref, wv_ref, wo_ref, wz_ref, hnw_ref,
                       c0_ref, n0_ref, m0_ref, *rest, layer, lseq):
    a_ref, c_ref, n_ref, m_ref, q_s, k_s, v_s, h_s, ig_s, lf_s = rest[-10:]
    hd = pl.program_id(0)
    b = pl.program_id(1)

    @pl.when(b == 0)
    def _project():
        hb = h_ref[...]
        q_s[...] = _dot(hb, wq_ref[0])
        k_s[...] = _dot(hb, wk_ref[0]) * (DH ** -0.5)
        v_s[...] = _dot(hb, wv_ref[0])
        ig_c, lf_c = _gate_columns(g_ref[...], hd, bif_ref[layer, hd], bif_ref[layer, N_HEADS + hd])
        ig_s[...] = ig_c
        lf_s[...] = lf_c

    rows = pl.ds(pl.multiple_of(b * lseq, lseq), lseq)
    h, c_new, n_new, m_new = _recurrence(q_s[rows, :], k_s[rows, :], v_s[rows, :], ig_s[rows, :], lf_s[rows, :],
                                         c0_ref[0, 0, 0], n0_ref[0, 0, 0], m0_ref[0, 0, 0][:, 0:1])
    c_ref[0, 0, 0] = c_new
    n_ref[0, 0] = n_new
    m_ref[0, 0] = jnp.broadcast_to(m_new, (1, LANES))
    h_s[rows, :] = h

    @pl.when(b == pl.num_programs(1) - 1)
    def _finish():
        hb = h_ref[...]
        a_ref[...] = _head_out(_dot(hb, wo_ref[0]), _dot(hb, wz_ref[0]), h_s[...], hnw_ref[0])


def _mlstm_seqs(h2, gates, w5, b_if, hnw, state_c, state_n, state_m, c_stack, *, depth, layer, batch, lseq):
    rows = batch * lseq
    in_specs = ([pl.BlockSpec(memory_space=pltpu.SMEM),
                 pl.BlockSpec((rows, D_MODEL), lambda hd, b: (0, 0)),
                 pl.BlockSpec((rows, LANES), lambda hd, b: (0, 0))]
                + _w5_specs(layer, lambda l, g, hd, b: (l, 0, g * N_HEADS + hd))
                + [pl.BlockSpec((1, 1, DH), lambda hd, b: (layer, 0, hd)),
                   pl.BlockSpec((1, 1, 1, DH, DH), lambda hd, b: (layer, b, hd, 0, 0)),
                   pl.BlockSpec((1, 1, 1, 1, DH), lambda hd, b: (layer, b, hd, 0, 0)),
                   pl.BlockSpec((1, 1, 1, 1, LANES), lambda hd, b: (layer, b, hd, 0, 0))])
    st_specs, st_shapes, extra_specs, extra_args, aliases = _state_out(
        depth, batch, layer, c_stack, lambda hd, b: (b, hd, 0, 0), len(in_specs))
    return pl.pallas_call(
        functools.partial(_mlstm_seqs_kernel, layer=layer, lseq=lseq),
        grid=(N_HEADS, batch),
        in_specs=in_specs + extra_specs,
        out_specs=[pl.BlockSpec((rows, DH), lambda hd, b: (0, hd))] + st_specs,
        out_shape=[jax.ShapeDtypeStruct((rows, D_MODEL), BF)] + st_shapes,
        input_output_aliases=aliases,
        scratch_shapes=[pltpu.VMEM((rows, DH), F32)] * 4 + [pltpu.VMEM((rows, 1), F32)] * 2,
        compiler_params=_params(2), name="mlstm_seqs",
    )(b_if, h2, gates, w5, w5, w5, w5, w5, hnw, state_c, state_n, state_m, *extra_args)


def _pool_kernel(*refs, lseq, nseq, pos0, has_state):
    if has_state:
        h_ref, w_ref, wmix_ref, ps_ref, hist0_ref, bb_ref, sga_ref, sgb_ref, hist_ref, uext = refs
    else:
        h_ref, w_ref, wmix_ref, ps_ref, bb_ref, sga_ref, sgb_ref, hist_ref, uext = refs
    i = pl.program_id(1)
    seg = HIST_ROWS + lseq

    @pl.when(i == 0)
    def _first():
        for sq in range(nseq):
            uext[sq * seg:sq * seg + HIST_ROWS, :] = (hist0_ref[0, sq] if has_state
                                                       else jnp.zeros((HIST_ROWS, D_B), F32))

    @pl.when(i > 0)
    def _carry():
        uext[0:HIST_ROWS, :] = uext[lseq:seg, :]

    hb = h_ref[...]
    u = _dot(hb, w_ref[0, :, 0:D_B])
    for sq in range(nseq):
        uext[sq * seg + HIST_ROWS:(sq + 1) * seg, :] = u[sq * lseq:(sq + 1) * lseq, :]
    zb = _dot(hb, w_ref[0, :, D_B:2 * D_B])
    sga_ref[...] = jax.nn.sigmoid(_dot(hb, w_ref[0, :, 2 * D_B:2 * D_B + D_MODEL]))
    sgb_ref[...] = jax.nn.sigmoid(_dot(hb, w_ref[0, :, 2 * D_B + D_MODEL:]))

    in_seq = lax.broadcasted_iota(jnp.int32, (lseq, 1), 0)
    pos = jnp.concatenate([in_seq] * nseq, axis=0) + (i * lseq + pos0 + 1)
    for g, w in enumerate(POOL_WINDOWS):
        sl = slice(g * POOL_GROUP, (g + 1) * POOL_GROUP)
        acc = uext[:, sl]
        shift = 1
        while shift < w:
            acc = acc + pltpu.roll(acc, shift, axis=0)
            shift *= 2
        win = jnp.concatenate([acc[sq * seg + HIST_ROWS:(sq + 1) * seg, :] for sq in range(nseq)], axis=0)
        pooled = win / jnp.minimum(pos, w).astype(F32)
        dlt = (pooled - u[:, sl]).astype(BF)
        mixed = _dot(dlt, wmix_ref[0, g * POOL_GROUP:(g + 1) * POOL_GROUP, :]) * ps_ref[0][:, sl]
        bb_ref[:, sl] = (_silu(zb[:, sl]) * mixed).astype(BF)

    @pl.when(i == pl.num_programs(1) - 1)
    def _hist():
        for sq in range(nseq):
            hist_ref[sq] = uext[sq * seg + lseq:(sq + 1) * seg, :]


def _pool(h2, w_rest, wmix, ps, hist0, *, layer, groups, tiles, lseq, nseq, pos0):
    t = lseq * nseq
    rows = groups * tiles * t
    has_state = hist0 is not None
    in_specs = [pl.BlockSpec((t, D_MODEL), lambda b, i: (b * tiles + i, 0)),
                _layer_spec(w_rest.shape, layer), _layer_spec(wmix.shape, layer), _layer_spec(ps.shape, layer)]
    args = [h2, w_rest, wmix, ps]
    if has_state:
        in_specs.append(pl.BlockSpec((1, nseq, HIST_ROWS, D_B), lambda b, i: (layer, b, 0, 0)))
        args.append(hist0)
    row_spec = lambda width: pl.BlockSpec((t, width), lambda b, i: (b * tiles + i, 0))
    return pl.pallas_call(
        functools.partial(_pool_kernel, lseq=lseq, nseq=nseq, pos0=pos0, has_state=has_state),
        grid=(groups, tiles),
        in_specs=in_specs,
        out_specs=[row_spec(D_B), row_spec(D_MODEL), row_spec(D_MODEL),
                   pl.BlockSpec((nseq, HIST_ROWS, D_B), lambda b, i: (b, 0, 0))],
        out_shape=[jax.ShapeDtypeStruct((rows, D_B), BF),
                   jax.ShapeDtypeStruct((rows, D_MODEL), F32),
                   jax.ShapeDtypeStruct((rows, D_MODEL), F32),
                   jax.ShapeDtypeStruct((groups * nseq, HIST_ROWS, D_B), F32)],
        scratch_shapes=[pltpu.VMEM((nseq * (HIST_ROWS + lseq), D_B), F32)],
        compiler_params=_params(2), name="pool",
    )(*args)


def _out_kernel(*refs, last, tiles_per_seq, nseq, lseq):
    if last:
        x_ref, a_ref, bb_ref, sga_ref, sgb_ref, gate_ref, wa_ref, wb_ref, wo_ref, nw_ref, y_ref = refs
    else:
        (x_ref, a_ref, bb_ref, sga_ref, sgb_ref, gate_ref, wa_ref, wb_ref, wo_ref, nw_ref,
         shift_ref, scl_ref, wif_ref, y_ref, hn_ref, g_ref) = refs
    seq = pl.program_id(0) // tiles_per_seq
    br_a = _dot(a_ref[...], wa_ref[0])
    br_b = _dot(bb_ref[...], wb_ref[0])
    merged = (sga_ref[...] * br_a + sgb_ref[...] * br_b).astype(BF)
    y = x_ref[...] + _mod_vec(gate_ref, seq, nseq, lseq) * _dot(merged, wo_ref[0])
    if last:
        r = lax.rsqrt(jnp.mean(y * y, axis=-1, keepdims=True) + EPS)
        y_ref[...] = y * r * nw_ref[0]
    else:
        y_ref[...] = y
        hn = _norm_mod(y, nw_ref[0], _mod_vec(scl_ref, seq, nseq, lseq),
                       _mod_vec(shift_ref, seq, nseq, lseq)).astype(BF)
        hn_ref[...] = hn
        g_ref[...] = _dot(hn, wif_ref[0])


def _out(x2, a2, bb2, sga, sgb, mod_t, wa, wb, wo, nw_all, wif, *, layer, group, tiles_per_seq, t, nseq, lseq,
         last):
    rows = x2.shape[0]
    row_spec = lambda width: pl.BlockSpec((t, width), lambda i: (i, 0))
    in_specs = [row_spec(D_MODEL), row_spec(D_MODEL), row_spec(D_B), row_spec(D_MODEL), row_spec(D_MODEL),
                _mod_spec(layer, 2, group),
                _layer_spec(wa.shape, layer), _layer_spec(wb.shape, layer), _layer_spec(wo.shape, layer),
                pl.BlockSpec((1, 1, D_MODEL), lambda i: (layer + 1, 0, 0))]
    args = [x2, a2, bb2, sga, sgb, mod_t, wa, wb, wo, nw_all]
    if last:
        out_specs = [row_spec(D_MODEL)]
        out_shape = [jax.ShapeDtypeStruct((rows, D_MODEL), F32)]
    else:
        in_specs += [_mod_spec(layer + 1, 0, group), _mod_spec(layer + 1, 1, group),
                     pl.BlockSpec((1, D_MODEL, LANES), lambda i: (layer + 1, 0, 0))]
        args += [mod_t, mod_t, wif]
        out_specs = [row_spec(D_MODEL), row_spec(D_MODEL), row_spec(LANES)]
        out_shape = [jax.ShapeDtypeStruct((rows, D_MODEL), F32),
                     jax.ShapeDtypeStruct((rows, D_MODEL), BF),
                     jax.ShapeDtypeStruct((rows, LANES), F32)]
    return pl.pallas_call(
        functools.partial(_out_kernel, last=last, tiles_per_seq=tiles_per_seq, nseq=nseq, lseq=lseq),
        grid=(rows // t,),
        in_specs=in_specs, out_specs=out_specs, out_shape=out_shape,
        compiler_params=_params(1), name="out",
    )(*args)


def kernel(x_prompt, x_sample, c_prompt, c_sample, state_C, state_n, state_m, state_pool,
           norm_w, w_ada, b_ada, w_in, b_if, head_norm_w, w_pool_mix, pool_scale,
           w_branch_a, w_branch_b, w_out, final_norm_w):
    depth = w_in.shape[0]
    bp, lp, _ = x_prompt.shape
    bs, ls, _ = x_sample.shape
    tp = 256
    t_mlstm = 4 * tp
    assert bp <= SUBLANES and bs == SUBLANES and lp % t_mlstm == 0 and ls >= POOL_HIST and ls % SUBLANES == 0

    w5, w_rest, wif = _cast_w_in(jnp.swapaxes(w_in, 1, 2))
    wa = _cast(w_branch_a, D_MODEL, 512)
    wb = _cast(w_branch_b, D_MODEL, 512)
    wo = _cast(w_out, D_MODEL, 512)
    wmix = _cast(w_pool_mix.reshape(depth, D_B, POOL_GROUP), POOL_GROUP, D_B)

    c_all = jnp.zeros((MOD_ROWS, D_MODEL), F32).at[:bp].set(c_prompt).at[SUBLANES:].set(c_sample)
    mod = _modulation(c_all, w_ada, b_ada)
    mod_t = mod.reshape(depth, MOD_ROWS, 3, D_MODEL).transpose(0, 2, 1, 3)

    nw_all = jnp.concatenate([norm_w, final_norm_w[None, :]], axis=0).reshape(depth + 1, 1, D_MODEL)
    hnw = head_norm_w.reshape(depth, 1, D_MODEL)
    ps = pool_scale.reshape(depth, 1, D_B)
    st_n = state_n.reshape(depth, bs, N_HEADS, 1, DH)
    st_m = jnp.broadcast_to(state_m[..., None, None], (depth, bs, N_HEADS, 1, LANES))
    hist0 = jnp.pad(state_pool, ((0, 0), (0, 0), (HIST_ROWS - POOL_HIST, 0), (0, 0)))

    xp = x_prompt.reshape(bp * lp, D_MODEL)
    xs = x_sample.reshape(bs * ls, D_MODEL)
    prompt = dict(group=0, tiles_per_seq=lp // tp, t=tp, nseq=1, lseq=tp)
    sample = dict(group=1, tiles_per_seq=1, t=bs * ls, nseq=bs, lseq=ls)
    hp, gp = _first_h(xp, nw_all, mod_t, wif, **prompt)
    hs, gs = _first_h(xs, nw_all, mod_t, wif, **sample)

    outs = {k: [] for k in ("np", "mp", "pp", "ns", "ms", "ps")}
    c1 = c2 = None
    for l in range(depth):
        last = l == depth - 1
        a_p, c1, n1, m1 = _mlstm_prompt(hp, gp, w5, b_if, hnw, c1, depth=depth, layer=l, batch=bp,
                                        seq_rows=lp, t=t_mlstm, chunk=tp)
        bb_p, sga_p, sgb_p, hist_p = _pool(hp, w_rest, wmix, ps, None, layer=l, groups=bp, tiles=lp // tp,
                                           lseq=tp, nseq=1, pos0=0)
        res = _out(xp, a_p, bb_p, sga_p, sgb_p, mod_t, wa, wb, wo, nw_all, wif, layer=l, last=last, **prompt)
        if last:
            (yp,) = res
        else:
            xp, hp, gp = res
        outs["np"].append(n1[:, :, 0, :])
        outs["mp"].append(m1[:, :, 0, 0])
        outs["pp"].append(hist_p[:, HIST_ROWS - POOL_HIST:, :])

        a_s, c2, n2, m2 = _mlstm_seqs(hs, gs, w5, b_if, hnw, state_C, st_n, st_m, c2, depth=depth, layer=l,
                                      batch=bs, lseq=ls)
        bb_s, sga_s, sgb_s, hist_s = _pool(hs, w_rest, wmix, ps, hist0, layer=l, groups=1, tiles=1,
                                           lseq=ls, nseq=bs, pos0=PAST_LEN)
        res = _out(xs, a_s, bb_s, sga_s, sgb_s, mod_t, wa, wb, wo, nw_all, wif, layer=l, last=last, **sample)
        if last:
            (ys,) = res
        else:
            xs, hs, gs = res
        outs["ns"].append(n2[:, :, 0, :])
        outs["ms"].append(m2[:, :, 0, 0])
        outs["ps"].append(hist_s[:, HIST_ROWS - POOL_HIST:, :])

    st = {k: jnp.stack(v) for k, v in outs.items()}
    return (yp.reshape(bp, lp, D_MODEL), ys.reshape(bs, ls, D_MODEL),
            c1, st["np"], st["mp"], st["pp"],
            c2, st["ns"], st["ms"], st["ps"])
```

```python
import functools

import jax
import jax.numpy as jnp
from jax import lax
from jax.experimental import pallas as pl
from jax.experimental.pallas import tpu as pltpu

D_MODEL = 2048
N_HEADS = 4
DH = D_MODEL // N_HEADS
D_B = D_MODEL // 2
POOL_WINDOWS = (2, 4, 8, 16)
POOL_GROUP = D_B // len(POOL_WINDOWS)
POOL_HIST = 15
HIST_ROWS = 16
PAST_LEN = 2048
EPS = 1e-6
NEG = -1e30
N_QKVO = 4 * D_MODEL
N_QKVOZ = 5 * D_MODEL
N_IF = 2 * N_HEADS
N_REST = 2 * D_B + 2 * D_MODEL

LANES = 128
SUBLANES = 8
VMEM_LIMIT = 60 * 1024 * 1024
MOD_ROWS = 2 * SUBLANES

BF = jnp.bfloat16
F32 = jnp.float32

_dot = functools.partial(jnp.dot, preferred_element_type=F32)


def _dot_nt(a, b):
    return lax.dot_general(a, b, (((1,), (1,)), ((), ())), preferred_element_type=F32)


def _dot_tn(a, b):
    return lax.dot_general(a, b, (((0,), (0,)), ((), ())), preferred_element_type=F32)


def _params(n_axes):
    return pltpu.CompilerParams(dimension_semantics=("arbitrary",) * n_axes,
                                vmem_limit_bytes=VMEM_LIMIT)


def _layer_spec(shape, layer):
    nd = len(shape)
    return pl.BlockSpec((1,) + tuple(shape[1:]), lambda *_: (layer,) + (0,) * (nd - 1),
                        pipeline_mode=pl.Buffered(1))


def _silu(x):
    return x * jax.nn.sigmoid(x)


def _log_sigmoid(x):
    return jnp.minimum(x, 0.0) - jnp.log(1.0 + jnp.exp(-jnp.abs(x)))


def _cast_kernel(w_ref, o_ref):
    o_ref[...] = w_ref[...].astype(BF)


def _cast_first_layer(w, rows_blk, first_row=0, nrows=None):
    _, kdim, n = w.shape
    nrows = nrows or kdim
    return pl.pallas_call(
        _cast_kernel, grid=(nrows // rows_blk,),
        in_specs=[pl.BlockSpec((1, rows_blk, n), lambda i: (0, first_row // rows_blk + i, 0))],
        out_specs=pl.BlockSpec((1, rows_blk, n), lambda i: (0, i, 0)),
        out_shape=jax.ShapeDtypeStruct((1, nrows, n), BF),
        compiler_params=_params(1), name="cast")(w)


def _tcast_shift_kernel(a_ref, b_ref, o_ref):
    rows = jnp.concatenate([a_ref[0][N_IF:, :], b_ref[0]], axis=0)
    o_ref[0] = rows.T.astype(BF)


def _tcast_gate_kernel(w_ref, o_ref):
    rows = jnp.concatenate([w_ref[0], jnp.zeros((LANES - N_IF, D_MODEL), F32)], axis=0)
    o_ref[0] = rows.T.astype(BF)


def _cast_pool_side(w_in_t):
    depth = w_in_t.shape[0]
    nb = 1024
    first = N_QKVOZ // nb
    w_rest = pl.pallas_call(
        _tcast_shift_kernel, grid=(depth, N_REST // nb),
        in_specs=[pl.BlockSpec((1, nb, D_MODEL), lambda l, j: (l, first + j, 0)),
                  pl.BlockSpec((1, N_IF, D_MODEL), lambda l, j: (l, (first + j + 1) * (nb // N_IF), 0))],
        out_specs=pl.BlockSpec((1, D_MODEL, nb), lambda l, j: (l, 0, j)),
        out_shape=jax.ShapeDtypeStruct((depth, D_MODEL, N_REST), BF),
        compiler_params=_params(2), name="cast_rest")(w_in_t, w_in_t)
    wif = pl.pallas_call(
        _tcast_gate_kernel, grid=(depth,),
        in_specs=[pl.BlockSpec((1, N_IF, D_MODEL), lambda l: (l, N_QKVOZ // N_IF, 0))],
        out_specs=pl.BlockSpec((1, D_MODEL, LANES), lambda l: (l, 0, 0)),
        out_shape=jax.ShapeDtypeStruct((depth, D_MODEL, LANES), BF),
        compiler_params=_params(1), name="cast_gates")(w_in_t)
    return w_rest, wif


def _mod_kernel(c_ref, w_ref, b_ref, o_ref):
    @pl.when(pl.program_id(1) == 0)
    def _init():
        o_ref[0] = jnp.broadcast_to(b_ref[0], o_ref.shape[1:])

    o_ref[0] += _dot(_silu(c_ref[...]).astype(BF), w_ref[0].astype(BF))


def _modulation(c_all, w_ada, b_ada):
    depth, kdim, n = w_ada.shape
    tk = 512
    return pl.pallas_call(
        _mod_kernel,
        grid=(depth, kdim // tk),
        in_specs=[pl.BlockSpec((MOD_ROWS, tk), lambda l, k: (0, k)),
                  pl.BlockSpec((1, tk, n), lambda l, k: (l, k, 0)),
                  pl.BlockSpec((1, 1, n), lambda l, k: (l, 0, 0))],
        out_specs=pl.BlockSpec((1, MOD_ROWS, n), lambda l, k: (l, 0, 0)),
        out_shape=jax.ShapeDtypeStruct((depth, MOD_ROWS, n), F32),
        compiler_params=_params(2), name="modulation",
    )(c_all, w_ada, b_ada.reshape(depth, 1, n))


def _mod_vec(ref, seq, nseq, lseq):
    if nseq == 1:
        return ref[0, 0, pl.ds(seq, 1), :]
    blk = ref[0, 0]
    return jnp.concatenate([jnp.broadcast_to(blk[s:s + 1, :], (lseq, D_MODEL)) for s in range(nseq)], axis=0)


def _mod_spec(layer, kind, group):
    return pl.BlockSpec((1, 1, SUBLANES, D_MODEL), lambda i: (layer, kind, group, 0))


def _norm_mod(y, nw, scl, shift):
    r = lax.rsqrt(jnp.mean(y * y, axis=-1, keepdims=True) + EPS)
    return (y * r * nw) * (1.0 + scl) + shift


def _h_kernel(x_ref, nw_ref, shift_ref, scl_ref, wif_ref, h_ref, g_ref, *, tiles_per_seq, nseq, lseq):
    seq = pl.program_id(0) // tiles_per_seq
    hn = _norm_mod(x_ref[...], nw_ref[0], _mod_vec(scl_ref, seq, nseq, lseq),
                   _mod_vec(shift_ref, seq, nseq, lseq)).astype(BF)
    h_ref[...] = hn
    g_ref[...] = _dot(hn, wif_ref[0])


def _first_h(x2, nw_all, mod_t, wif, *, group, tiles_per_seq, t, nseq, lseq):
    rows = x2.shape[0]
    return pl.pallas_call(
        functools.partial(_h_kernel, tiles_per_seq=tiles_per_seq, nseq=nseq, lseq=lseq),
        grid=(rows // t,),
        in_specs=[pl.BlockSpec((t, D_MODEL), lambda i: (i, 0)),
                  pl.BlockSpec((1, 1, D_MODEL), lambda i: (0, 0, 0)),
                  _mod_spec(0, 0, group), _mod_spec(0, 1, group),
                  pl.BlockSpec((1, D_MODEL, LANES), lambda i: (0, 0, 0))],
        out_specs=[pl.BlockSpec((t, D_MODEL), lambda i: (i, 0)),
                   pl.BlockSpec((t, LANES), lambda i: (i, 0))],
        out_shape=[jax.ShapeDtypeStruct((rows, D_MODEL), BF),
                   jax.ShapeDtypeStruct((rows, LANES), F32)],
        compiler_params=_params(1), name="first_h",
    )(x2, nw_all, mod_t, mod_t, wif)


def _gate_columns(g, hd, b_i, b_f):
    lane = lax.broadcasted_iota(jnp.int32, g.shape, 1)
    ig = jnp.sum(jnp.where(lane == hd, g, 0.0), axis=1, keepdims=True) + b_i
    fg = jnp.sum(jnp.where(lane == hd + N_HEADS, g, 0.0), axis=1, keepdims=True) + b_f
    return ig, _log_sigmoid(fg)


def _recurrence(q, k, v, ig_c, lf_c, c_old, n_old, m0):
    t = q.shape[0]
    qb = q.astype(BF)
    vb = v.astype(BF)
    row = lax.broadcasted_iota(jnp.int32, (t, t), 0)
    col = lax.broadcasted_iota(jnp.int32, (t, t), 1)
    causal = col <= row
    eye = col == row
    ig_r = jnp.sum(jnp.where(eye, ig_c, 0.0), axis=0, keepdims=True)
    lf_r = jnp.sum(jnp.where(eye, lf_c, 0.0), axis=0, keepdims=True)
    b_c = jnp.sum(jnp.where(causal, lf_r, 0.0), axis=1, keepdims=True)
    b_r = jnp.sum(jnp.where(row <= col, lf_c, 0.0), axis=0, keepdims=True)
    g_r = ig_r - b_r
    g_c = ig_c - b_c
    big_m = jnp.maximum(jnp.max(jnp.where(causal, g_r, NEG), axis=1, keepdims=True), m0)
    w_intra = jnp.exp(jnp.where(causal, g_r - big_m, NEG))
    a_inter = jnp.exp(m0 - big_m)

    s = _dot_nt(qb, k.astype(BF)) * w_intra
    num = a_inter * _dot(qb, c_old.astype(BF)) + _dot(s.astype(BF), vb)
    den = a_inter * jnp.sum(q * n_old, axis=1, keepdims=True) + jnp.sum(s, axis=1, keepdims=True)
    h = num * (1.0 / jnp.maximum(jnp.abs(den), jnp.exp(-(b_c + big_m))))

    m_last = big_m[t - 1:t, :]
    kw = k * jnp.exp(g_c - m_last)
    decay = jnp.exp(m0 - m_last)
    c_new = decay * c_old + _dot_tn(kw.astype(BF), vb)
    n_new = decay * n_old + jnp.sum(kw, axis=0, keepdims=True)
    return h, c_new, n_new, b_c[t - 1:t, :] + m_last


def _head_out(o, z, h, hnw):
    og = jax.nn.sigmoid(o) * h
    mu = jnp.mean(og, axis=1, keepdims=True)
    dev = og - mu
    var = jnp.mean(dev * dev, axis=1, keepdims=True)
    return (_silu(z) * (dev * lax.rsqrt(var + EPS) * hnw)).astype(BF)


def _mlstm_kernel(*refs, layer, chunk, aliased, convert):
    bif_ref, h_ref, g_ref, wq_ref, wk_ref, wv_ref, wo_ref, wz_ref, hnw_ref = refs[:9]
    refs = refs[9 + aliased:]
    if convert:
        fa_ref, fb_ref, a_ref, c_ref, n_ref, m_ref, na_ref, nb_ref = refs
        na_ref[...] = fa_ref[...].astype(BF)
        nb_ref[...] = fb_ref[...].astype(BF)
    else:
        a_ref, c_ref, n_ref, m_ref = refs
    hd = pl.program_id(0)

    @pl.when(pl.program_id(2) == 0)
    def _init():
        c_ref[0, 0, 0] = jnp.zeros((DH, DH), F32)
        n_ref[0, 0] = jnp.zeros((1, DH), F32)
        m_ref[0, 0] = jnp.zeros((1, LANES), F32)

    hb = h_ref[...]
    q = _dot_nt(hb, wq_ref[0])
    k = _dot_nt(hb, wk_ref[0]) * (DH ** -0.5)
    v = _dot_nt(hb, wv_ref[0])
    ig_c, lf_c = _gate_columns(g_ref[...], hd, bif_ref[layer, hd], bif_ref[layer, N_HEADS + hd])
    c, n, m = c_ref[0, 0, 0], n_ref[0, 0], m_ref[0, 0][:, 0:1]
    hs = []
    for j in range(hb.shape[0] // chunk):
        r = slice(j * chunk, (j + 1) * chunk)
        h, c, n, m = _recurrence(q[r], k[r], v[r], ig_c[r], lf_c[r], c, n, m)
        hs.append(h)
    c_ref[0, 0, 0] = c
    n_ref[0, 0] = n
    m_ref[0, 0] = jnp.broadcast_to(m, (1, LANES))
    a_ref[...] = _head_out(_dot_nt(hb, wo_ref[0]), _dot_nt(hb, wz_ref[0]), jnp.concatenate(hs, axis=0), hnw_ref[0])


def _w5_specs(head_of):
    qkvo = [pl.BlockSpec((1, DH, D_MODEL), lambda *g, grp=grp: (0, grp * N_HEADS + head_of(*g), 0))
            for grp in range(4)]
    return qkvo + [pl.BlockSpec((1, DH, D_MODEL), lambda *g: (0, head_of(*g), 0))]


def _state_out(depth, batch, layer, c_stack, index_map, n_inputs, per_step=1):
    specs = [pl.BlockSpec((1, per_step, 1, DH, DH), lambda *g: (layer,) + index_map(*g)),
             pl.BlockSpec((per_step, 1, 1, DH), index_map),
             pl.BlockSpec((per_step, 1, 1, LANES), index_map)]
    shapes = [jax.ShapeDtypeStruct((depth, batch, N_HEADS, DH, DH), F32),
              jax.ShapeDtypeStruct((batch, N_HEADS, 1, DH), F32),
              jax.ShapeDtypeStruct((batch, N_HEADS, 1, LANES), F32)]
    if c_stack is None:
        return specs, shapes, [], [], {}
    return specs, shapes, [pl.BlockSpec(memory_space=pl.ANY)], [c_stack], {n_inputs: 1}


def _mlstm_prompt(h2, gates, wts, b_if, hnw, c_stack, w_in_t, *, depth, layer, batch, seq_rows, t, chunk):
    nck = seq_rows // t
    rows = batch * seq_rows
    steps = N_HEADS * batch * nck
    convert = layer + 1 < depth
    in_specs = ([pl.BlockSpec(memory_space=pltpu.SMEM),
                 pl.BlockSpec((t, D_MODEL), lambda hd, b, c: (b * nck + c, 0)),
                 pl.BlockSpec((t, LANES), lambda hd, b, c: (b * nck + c, 0))]
                + _w5_specs(lambda hd, b, c: hd)
                + [pl.BlockSpec((1, 1, DH), lambda hd, b, c: (layer, 0, hd))])
    args = [b_if, h2, gates, wts["w5a"], wts["w5a"], wts["w5a"], wts["w5a"], wts["w5b"], hnw]
    st_specs, st_shapes, extra_specs, extra_args, aliases = _state_out(
        depth, batch, layer, c_stack, lambda hd, b, c: (b, hd, 0, 0), len(in_specs))
    in_specs += extra_specs
    args += extra_args
    out_specs = [pl.BlockSpec((t, DH), lambda hd, b, c: (b * nck + c, hd))] + st_specs
    out_shape = [jax.ShapeDtypeStruct((rows, D_MODEL), BF)] + st_shapes
    if convert:
        assert N_QKVO % (2 * SUBLANES * steps) == 0 and D_MODEL % (2 * SUBLANES * steps) == 0
        ra, rb = N_QKVO // steps, D_MODEL // steps
        step = lambda hd, b, c: (hd * batch + b) * nck + c
        in_specs += [pl.BlockSpec((1, ra, D_MODEL), lambda *g: (layer + 1, step(*g), 0)),
                     pl.BlockSpec((1, rb, D_MODEL), lambda *g: (layer + 1, N_QKVO // rb + step(*g), 0))]
        args += [w_in_t, w_in_t]
        out_specs += [pl.BlockSpec((1, ra, D_MODEL), lambda *g: (0, step(*g), 0)),
                      pl.BlockSpec((1, rb, D_MODEL), lambda *g: (0, step(*g), 0))]
        out_shape += [jax.ShapeDtypeStruct((1, N_QKVO, D_MODEL), BF),
                      jax.ShapeDtypeStruct((1, D_MODEL, D_MODEL), BF)]
    return pl.pallas_call(
        functools.partial(_mlstm_kernel, layer=layer, chunk=chunk, aliased=c_stack is not None, convert=convert),
        grid=(N_HEADS, batch, nck),
        in_specs=in_specs, out_specs=out_specs, out_shape=out_shape,
        input_output_aliases=aliases,
        compiler_params=_params(3), name="mlstm",
    )(*args)


def _mlstm_seqs_kernel(bif_ref, h_ref, g_ref, wq_ref, wk_ref, wv_ref, wo_ref, wz_ref, hnw_ref,
                       c0_ref, n0_ref, m0_ref, *rest, layer, lseq, per_step):
    a_ref, c_ref, n_ref, m_ref, q_s, k_s, v_s, h_s, ig_s, lf_s = rest[-10:]
    hd = pl.program_id(0)
    b = pl.program_id(1)

    @pl.when(b == 0)
    def _project():
        hb = h_ref[...]
        q_s[...] = _dot_nt(hb, wq_ref[0])
        k_s[...] = _dot_nt(hb, wk_ref[0]) * (DH ** -0.5)
        v_s[...] = _dot_nt(hb, wv_ref[0])
        ig_c, lf_c = _gate_columns(g_ref[...], hd, bif_ref[layer, hd], bif_ref[layer, N_HEADS + hd])
        ig_s[...] = ig_c
        lf_s[...] = lf_c

    for j in range(per_step):
        rows = pl.ds(pl.multiple_of((b * per_step + j) * lseq, lseq), lseq)
        h, c_new, n_new, m_new = _recurrence(q_s[rows, :], k_s[rows, :], v_s[rows, :], ig_s[rows, :],
                                             lf_s[rows, :], c0_ref[0, j, 0], n0_ref[0, j, 0],
                                             m0_ref[0, j, 0][:, 0:1])
        c_ref[0, j, 0] = c_new
        n_ref[j, 0] = n_new
        m_ref[j, 0] = jnp.broadcast_to(m_new, (1, LANES))
        h_s[rows, :] = h

    @pl.when(b == pl.num_programs(1) - 1)
    def _finish():
        hb = h_ref[...]
        a_ref[...] = _head_out(_dot_nt(hb, wo_ref[0]), _dot_nt(hb, wz_ref[0]), h_s[...], hnw_ref[0])


def _mlstm_seqs(h2, gates, wts, b_if, hnw, state_c, state_n, state_m, c_stack, *, depth, layer, batch, lseq,
                per_step):
    rows = batch * lseq
    in_specs = ([pl.BlockSpec(memory_space=pltpu.SMEM),
                 pl.BlockSpec((rows, D_MODEL), lambda hd, b: (0, 0)),
                 pl.BlockSpec((rows, LANES), lambda hd, b: (0, 0))]
                + _w5_specs(lambda hd, b: hd)
                + [pl.BlockSpec((1, 1, DH), lambda hd, b: (layer, 0, hd)),
                   pl.BlockSpec((1, per_step, 1, DH, DH), lambda hd, b: (layer, b, hd, 0, 0)),
                   pl.BlockSpec((1, per_step, 1, 1, DH), lambda hd, b: (layer, b, hd, 0, 0)),
                   pl.BlockSpec((1, per_step, 1, 1, LANES), lambda hd, b: (layer, b, hd, 0, 0))])
    st_specs, st_shapes, extra_specs, extra_args, aliases = _state_out(
        depth, batch, layer, c_stack, lambda hd, b: (b, hd, 0, 0), len(in_specs), per_step)
    return pl.pallas_call(
        functools.partial(_mlstm_seqs_kernel, layer=layer, lseq=lseq, per_step=per_step),
        grid=(N_HEADS, batch // per_step),
        in_specs=in_specs + extra_specs,
        out_specs=[pl.BlockSpec((rows, DH), lambda hd, b: (0, hd))] + st_specs,
        out_shape=[jax.ShapeDtypeStruct((rows, D_MODEL), BF)] + st_shapes,
        input_output_aliases=aliases,
        scratch_shapes=[pltpu.VMEM((rows, DH), F32)] * 4 + [pltpu.VMEM((rows, 1), F32)] * 2,
        compiler_params=_params(2), name="mlstm_seqs",
    )(b_if, h2, gates, wts["w5a"], wts["w5a"], wts["w5a"], wts["w5a"], wts["w5b"], hnw,
      state_c, state_n, state_m, *extra_args)


def _pool_kernel(*refs, lseq, nseq, pos0, has_state):
    if has_state:
        h_ref, w_ref, wmix_ref, ps_ref, hist0_ref, bb_ref, sga_ref, sgb_ref, hist_ref, uext = refs
    else:
        h_ref, w_ref, wmix_ref, ps_ref, bb_ref, sga_ref, sgb_ref, hist_ref, uext = refs
    i = pl.program_id(1)
    seg = HIST_ROWS + lseq

    @pl.when(i == 0)
    def _first():
        for sq in range(nseq):
            uext[sq * seg:sq * seg + HIST_ROWS, :] = (hist0_ref[0, sq] if has_state
                                                       else jnp.zeros((HIST_ROWS, D_B), F32))

    @pl.when(i > 0)
    def _carry():
        uext[0:HIST_ROWS, :] = uext[lseq:seg, :]

    hb = h_ref[...]
    u = _dot(hb, w_ref[0, :, 0:D_B])
    for sq in range(nseq):
        uext[sq * seg + HIST_ROWS:(sq + 1) * seg, :] = u[sq * lseq:(sq + 1) * lseq, :]
    zb = _dot(hb, w_ref[0, :, D_B:2 * D_B])
    sga_ref[...] = jax.nn.sigmoid(_dot(hb, w_ref[0, :, 2 * D_B:2 * D_B + D_MODEL]))
    sgb_ref[...] = jax.nn.sigmoid(_dot(hb, w_ref[0, :, 2 * D_B + D_MODEL:]))

    in_seq = lax.broadcasted_iota(jnp.int32, (lseq, 1), 0)
    pos = jnp.concatenate([in_seq] * nseq, axis=0) + (i * lseq + pos0 + 1)
    for g, w in enumerate(POOL_WINDOWS):
        sl = slice(g * POOL_GROUP, (g + 1) * POOL_GROUP)
        acc = uext[:, sl]
        shift = 1
        while shift < w:
            acc = acc + pltpu.roll(acc, shift, axis=0)
            shift *= 2
        win = jnp.concatenate([acc[sq * seg + HIST_ROWS:(sq + 1) * seg, :] for sq in range(nseq)], axis=0)
        pooled = win / jnp.minimum(pos, w).astype(F32)
        dlt = (pooled - u[:, sl]).astype(BF)
        mixed = _dot(dlt, wmix_ref[0, g].astype(BF)) * ps_ref[0][:, sl]
        bb_ref[:, sl] = (_silu(zb[:, sl]) * mixed).astype(BF)

    @pl.when(i == pl.num_programs(1) - 1)
    def _hist():
        for sq in range(nseq):
            hist_ref[sq] = uext[sq * seg + lseq:(sq + 1) * seg, :]


def _pool(h2, w_rest, wmix, ps, hist0, *, layer, groups, tiles, lseq, nseq, pos0):
    t = lseq * nseq
    rows = groups * tiles * t
    has_state = hist0 is not None
    in_specs = [pl.BlockSpec((t, D_MODEL), lambda b, i: (b * tiles + i, 0)),
                _layer_spec(w_rest.shape, layer), _layer_spec(wmix.shape, layer), _layer_spec(ps.shape, layer)]
    args = [h2, w_rest, wmix, ps]
    if has_state:
        in_specs.append(pl.BlockSpec((1, nseq, HIST_ROWS, D_B), lambda b, i: (layer, b, 0, 0)))
        args.append(hist0)
    row_spec = lambda width: pl.BlockSpec((t, width), lambda b, i: (b * tiles + i, 0))
    return pl.pallas_call(
        functools.partial(_pool_kernel, lseq=lseq, nseq=nseq, pos0=pos0, has_state=has_state),
        grid=(groups, tiles),
        in_specs=in_specs,
        out_specs=[row_spec(D_B), row_spec(D_MODEL), row_spec(D_MODEL),
                   pl.BlockSpec((nseq, HIST_ROWS, D_B), lambda b, i: (b, 0, 0))],
        out_shape=[jax.ShapeDtypeStruct((rows, D_B), BF),
                   jax.ShapeDtypeStruct((rows, D_MODEL), F32),
                   jax.ShapeDtypeStruct((rows, D_MODEL), F32),
                   jax.ShapeDtypeStruct((groups * nseq, HIST_ROWS, D_B), F32)],
        scratch_shapes=[pltpu.VMEM((nseq * (HIST_ROWS + lseq), D_B), F32)],
        compiler_params=_params(2), name="pool",
    )(*args)


def _out_kernel(*refs, last, tiles_per_seq, nseq, lseq, convert):
    x_ref, a_ref, bb_ref, sga_ref, sgb_ref, gate_ref, wa_ref, wb_ref, wo_ref, nw_ref = refs[:10]
    refs = refs[10:]
    if not last:
        shift_ref, scl_ref, wif_ref = refs[:3]
        refs = refs[3:]
    if convert:
        fa_ref, fb_ref, fo_ref = refs[:3]
        refs = refs[3:]
        na_ref, nb_ref, no_ref = refs[-3:]
        na_ref[...] = fa_ref[...].astype(BF)
        nb_ref[...] = fb_ref[...].astype(BF)
        no_ref[...] = fo_ref[...].astype(BF)
    seq = pl.program_id(0) // tiles_per_seq
    br_a = _dot(a_ref[...], wa_ref[0])
    br_b = _dot(bb_ref[...], wb_ref[0])
    merged = (sga_ref[...] * br_a + sgb_ref[...] * br_b).astype(BF)
    y = x_ref[...] + _mod_vec(gate_ref, seq, nseq, lseq) * _dot(merged, wo_ref[0])
    if last:
        r = lax.rsqrt(jnp.mean(y * y, axis=-1, keepdims=True) + EPS)
        refs[0][...] = y * r * nw_ref[0]
    else:
        y_ref, hn_ref, g_ref = refs[:3]
        y_ref[...] = y
        hn = _norm_mod(y, nw_ref[0], _mod_vec(scl_ref, seq, nseq, lseq),
                       _mod_vec(shift_ref, seq, nseq, lseq)).astype(BF)
        hn_ref[...] = hn
        g_ref[...] = _dot(hn, wif_ref[0])


def _out(x2, a2, bb2, sga, sgb, mod_t, wts, nw_all, wif, f32_weights, *, layer, group, tiles_per_seq, t, nseq,
         lseq, last, convert):
    rows = x2.shape[0]
    steps = rows // t
    row_spec = lambda width: pl.BlockSpec((t, width), lambda i: (i, 0))
    in_specs = [row_spec(D_MODEL), row_spec(D_MODEL), row_spec(D_B), row_spec(D_MODEL), row_spec(D_MODEL),
                _mod_spec(layer, 2, group),
                _layer_spec(wts["wa"].shape, 0), _layer_spec(wts["wb"].shape, 0), _layer_spec(wts["wo"].shape, 0),
                pl.BlockSpec((1, 1, D_MODEL), lambda i: (layer + 1, 0, 0))]
    args = [x2, a2, bb2, sga, sgb, mod_t, wts["wa"], wts["wb"], wts["wo"], nw_all]
    if last:
        out_specs = [row_spec(D_MODEL)]
        out_shape = [jax.ShapeDtypeStruct((rows, D_MODEL), F32)]
    else:
        in_specs += [_mod_spec(layer + 1, 0, group), _mod_spec(layer + 1, 1, group),
                     pl.BlockSpec((1, D_MODEL, LANES), lambda i: (layer + 1, 0, 0))]
        args += [mod_t, mod_t, wif]
        out_specs = [row_spec(D_MODEL), row_spec(D_MODEL), row_spec(LANES)]
        out_shape = [jax.ShapeDtypeStruct((rows, D_MODEL), F32),
                     jax.ShapeDtypeStruct((rows, D_MODEL), BF),
                     jax.ShapeDtypeStruct((rows, LANES), F32)]
    if convert:
        for w in f32_weights:
            _, kdim, n = w.shape
            assert kdim % (2 * SUBLANES * steps) == 0
            in_specs.append(pl.BlockSpec((1, kdim // steps, n), lambda i: (layer + 1, i, 0)))
            args.append(w)
            out_specs.append(pl.BlockSpec((1, kdim // steps, n), lambda i: (0, i, 0)))
            out_shape.append(jax.ShapeDtypeStruct((1, kdim, n), BF))
    return pl.pallas_call(
        functools.partial(_out_kernel, last=last, tiles_per_seq=tiles_per_seq, nseq=nseq, lseq=lseq,
                          convert=convert),
        grid=(steps,),
        in_specs=in_specs, out_specs=out_specs, out_shape=out_shape,
        compiler_params=_params(1), name="out",
    )(*args)


def kernel(x_prompt, x_sample, c_prompt, c_sample, state_C, state_n, state_m, state_pool,
           norm_w, w_ada, b_ada, w_in, b_if, head_norm_w, w_pool_mix, pool_scale,
           w_branch_a, w_branch_b, w_out, final_norm_w):
    depth = w_in.shape[0]
    bp, lp, _ = x_prompt.shape
    bs, ls, _ = x_sample.shape
    tp = 256
    t_mlstm = 4 * tp
    assert bp <= SUBLANES and bs == SUBLANES and lp % t_mlstm == 0 and ls >= POOL_HIST and ls % SUBLANES == 0

    w_in_t = jnp.swapaxes(w_in, 1, 2)
    w_rest, wif = _cast_pool_side(w_in_t)
    wts = dict(w5a=_cast_first_layer(w_in_t, 1024, 0, N_QKVO),
               w5b=_cast_first_layer(w_in_t, 1024, N_QKVO, D_MODEL),
               wa=_cast_first_layer(w_branch_a, 512), wb=_cast_first_layer(w_branch_b, 512),
               wo=_cast_first_layer(w_out, 512))

    c_all = jnp.zeros((MOD_ROWS, D_MODEL), F32).at[:bp].set(c_prompt).at[SUBLANES:].set(c_sample)
    mod = _modulation(c_all, w_ada, b_ada)
    mod_t = mod.reshape(depth, MOD_ROWS, 3, D_MODEL).transpose(0, 2, 1, 3)

    nw_all = jnp.concatenate([norm_w, final_norm_w[None, :]], axis=0).reshape(depth + 1, 1, D_MODEL)
    hnw = head_norm_w.reshape(depth, 1, D_MODEL)
    ps = pool_scale.reshape(depth, 1, D_B)
    st_n = state_n.reshape(depth, bs, N_HEADS, 1, DH)
    st_m = jnp.broadcast_to(state_m[..., None, None], (depth, bs, N_HEADS, 1, LANES))
    hist0 = jnp.pad(state_pool, ((0, 0), (0, 0), (HIST_ROWS - POOL_HIST, 0), (0, 0)))

    xp = x_prompt.reshape(bp * lp, D_MODEL)
    xs = x_sample.reshape(bs * ls, D_MODEL)
    prompt = dict(group=0, tiles_per_seq=lp // tp, t=tp, nseq=1, lseq=tp)
    sample = dict(group=1, tiles_per_seq=1, t=bs * ls, nseq=bs, lseq=ls)
    hp, gp = _first_h(xp, nw_all, mod_t, wif, **prompt)
    hs, gs = _first_h(xs, nw_all, mod_t, wif, **sample)

    outs = {k: [] for k in ("np", "mp", "pp", "ns", "ms", "ps")}
    c1 = c2 = None
    for l in range(depth):
        last = l == depth - 1
        nxt = {}
        res = _mlstm_prompt(hp, gp, wts, b_if, hnw, c1, w_in_t, depth=depth, layer=l, batch=bp,
                            seq_rows=lp, t=t_mlstm, chunk=tp)
        a_p, c1, n1, m1 = res[:4]
        if not last:
            nxt["w5a"], nxt["w5b"] = res[4:]
        bb_p, sga_p, sgb_p, hist_p = _pool(hp, w_rest, w_pool_mix, ps, None, layer=l, groups=bp, tiles=lp // tp,
                                           lseq=tp, nseq=1, pos0=0)
        res = _out(xp, a_p, bb_p, sga_p, sgb_p, mod_t, wts, nw_all, wif, (w_branch_a, w_branch_b, w_out),
                   layer=l, last=last, convert=not last, **prompt)
        if last:
            (yp,) = res
        else:
            xp, hp, gp, nxt["wa"], nxt["wb"], nxt["wo"] = res
        outs["np"].append(n1[:, :, 0, :])
        outs["mp"].append(m1[:, :, 0, 0])
        outs["pp"].append(hist_p[:, HIST_ROWS - POOL_HIST:, :])

        a_s, c2, n2, m2 = _mlstm_seqs(hs, gs, wts, b_if, hnw, state_C, st_n, st_m, c2, depth=depth, layer=l,
                                      batch=bs, lseq=ls, per_step=4)
        bb_s, sga_s, sgb_s, hist_s = _pool(hs, w_rest, w_pool_mix, ps, hist0, layer=l, groups=1, tiles=1,
                                           lseq=ls, nseq=bs, pos0=PAST_LEN)
        res = _out(xs, a_s, bb_s, sga_s, sgb_s, mod_t, wts, nw_all, wif, None, layer=l, last=last,
                   convert=False, **sample)
        if last:
            (ys,) = res
        else:
            xs, hs, gs = res
        outs["ns"].append(n2[:, :, 0, :])
        outs["ms"].append(m2[:, :, 0, 0])
        outs["ps"].append(hist_s[:, HIST_ROWS - POOL_HIST:, :])
        wts = nxt

    st = {k: jnp.stack(v) for k, v in outs.items()}
    return (yp.reshape(bp, lp, D_MODEL), ys.reshape(bs, ls, D_MODEL),
            c1, st["np"], st["mp"], st["pp"],
            c2, st["ns"], st["ms"], st["ps"])
```

```python
import functools

import jax
import jax.numpy as jnp
from jax import lax
from jax.experimental import pallas as pl
from jax.experimental.pallas import tpu as pltpu

D_MODEL = 2048
N_HEADS = 4
DH = D_MODEL // N_HEADS
D_B = D_MODEL // 2
POOL_WINDOWS = (2, 4, 8, 16)
POOL_GROUP = D_B // len(POOL_WINDOWS)
POOL_HIST = 15
HIST_ROWS = 16
PAST_LEN = 2048
EPS = 1e-6
NEG = -1e30
N_QKVO = 4 * D_MODEL
N_QKVOZ = 5 * D_MODEL
N_IF = 2 * N_HEADS
N_REST = 2 * D_B + 2 * D_MODEL

LANES = 128
SUBLANES = 8
VMEM_LIMIT = 60 * 1024 * 1024
MOD_ROWS = 2 * SUBLANES

BF = jnp.bfloat16
F32 = jnp.float32

_dot = functools.partial(jnp.dot, preferred_element_type=F32)


def _dot_nt(a, b):
    return lax.dot_general(a, b, (((1,), (1,)), ((), ())), preferred_element_type=F32)


def _dot_tn(a, b):
    return lax.dot_general(a, b, (((0,), (0,)), ((), ())), preferred_element_type=F32)


def _params(n_axes):
    return pltpu.CompilerParams(dimension_semantics=("arbitrary",) * n_axes,
                                vmem_limit_bytes=VMEM_LIMIT)


def _layer_spec(shape, layer):
    nd = len(shape)
    return pl.BlockSpec((1,) + tuple(shape[1:]), lambda *_: (layer,) + (0,) * (nd - 1),
                        pipeline_mode=pl.Buffered(1))


def _silu(x):
    return x * jax.nn.sigmoid(x)


def _log_sigmoid(x):
    return jnp.minimum(x, 0.0) - jnp.log(1.0 + jnp.exp(-jnp.abs(x)))


def _cast_kernel(w_ref, o_ref):
    o_ref[...] = w_ref[...].astype(BF)


def _cast_first_layer(w, rows_blk, first_row=0, nrows=None):
    _, kdim, n = w.shape
    nrows = nrows or kdim
    return pl.pallas_call(
        _cast_kernel, grid=(nrows // rows_blk,),
        in_specs=[pl.BlockSpec((1, rows_blk, n), lambda i: (0, first_row // rows_blk + i, 0))],
        out_specs=pl.BlockSpec((1, rows_blk, n), lambda i: (0, i, 0)),
        out_shape=jax.ShapeDtypeStruct((1, nrows, n), BF),
        compiler_params=_params(1), name="cast")(w)


def _shifted_rows(a_rows, next_rows):
    return jnp.concatenate([a_rows[N_IF:, :], next_rows], axis=0)


def _shift_specs(layer, nb, block_of_step):
    first = N_QKVOZ // nb
    return [pl.BlockSpec((1, nb, D_MODEL), lambda *g: (layer, first + block_of_step(*g), 0)),
            pl.BlockSpec((1, N_IF, D_MODEL),
                         lambda *g: (layer, (first + block_of_step(*g) + 1) * (nb // N_IF), 0))]


def _cast_shift_kernel(a_ref, b_ref, o_ref):
    o_ref[0] = _shifted_rows(a_ref[0], b_ref[0]).astype(BF)


def _cast_gate_kernel(w_ref, o_ref):
    rows = jnp.concatenate([w_ref[0], jnp.zeros((LANES - N_IF, D_MODEL), F32)], axis=0)
    o_ref[0] = rows.T.astype(BF)


def _cast_pool_side(w_in_t):
    depth = w_in_t.shape[0]
    nb = 1024
    w_rest = pl.pallas_call(
        _cast_shift_kernel, grid=(N_REST // nb,),
        in_specs=_shift_specs(0, nb, lambda j: j),
        out_specs=pl.BlockSpec((1, nb, D_MODEL), lambda j: (0, j, 0)),
        out_shape=jax.ShapeDtypeStruct((1, N_REST, D_MODEL), BF),
        compiler_params=_params(1), name="cast_rest")(w_in_t, w_in_t)
    wif = pl.pallas_call(
        _cast_gate_kernel, grid=(depth,),
        in_specs=[pl.BlockSpec((1, N_IF, D_MODEL), lambda l: (l, N_QKVOZ // N_IF, 0))],
        out_specs=pl.BlockSpec((1, D_MODEL, LANES), lambda l: (l, 0, 0)),
        out_shape=jax.ShapeDtypeStruct((depth, D_MODEL, LANES), BF),
        compiler_params=_params(1), name="cast_gates")(w_in_t)
    return w_rest, wif


def _mod_kernel(c_ref, w_ref, b_ref, o_ref):
    @pl.when(pl.program_id(1) == 0)
    def _init():
        o_ref[0] = jnp.broadcast_to(b_ref[0], o_ref.shape[1:])

    o_ref[0] += _dot(_silu(c_ref[...]).astype(BF), w_ref[0].astype(BF))


def _modulation(c_all, w_ada, b_ada):
    depth, kdim, n = w_ada.shape
    tk = 512
    return pl.pallas_call(
        _mod_kernel,
        grid=(depth, kdim // tk),
        in_specs=[pl.BlockSpec((MOD_ROWS, tk), lambda l, k: (0, k)),
                  pl.BlockSpec((1, tk, n), lambda l, k: (l, k, 0)),
                  pl.BlockSpec((1, 1, n), lambda l, k: (l, 0, 0))],
        out_specs=pl.BlockSpec((1, MOD_ROWS, n), lambda l, k: (l, 0, 0)),
        out_shape=jax.ShapeDtypeStruct((depth, MOD_ROWS, n), F32),
        compiler_params=_params(2), name="modulation",
    )(c_all, w_ada, b_ada.reshape(depth, 1, n))


def _mod_vec(ref, seq, nseq, lseq):
    if nseq == 1:
        return ref[0, 0, pl.ds(seq, 1), :]
    blk = ref[0, 0]
    return jnp.concatenate([jnp.broadcast_to(blk[s:s + 1, :], (lseq, D_MODEL)) for s in range(nseq)], axis=0)


def _mod_spec(layer, kind, group):
    return pl.BlockSpec((1, 1, SUBLANES, D_MODEL), lambda *_: (layer, kind, group, 0))


def _norm_mod(y, nw, scl, shift):
    r = lax.rsqrt(jnp.mean(y * y, axis=-1, keepdims=True) + EPS)
    return (y * r * nw) * (1.0 + scl) + shift


def _h_kernel(x_ref, nw_ref, shift_ref, scl_ref, wif_ref, h_ref, g_ref, *, tiles_per_seq, nseq, lseq):
    seq = pl.program_id(0) // tiles_per_seq
    hn = _norm_mod(x_ref[...], nw_ref[0], _mod_vec(scl_ref, seq, nseq, lseq),
                   _mod_vec(shift_ref, seq, nseq, lseq)).astype(BF)
    h_ref[...] = hn
    g_ref[...] = _dot(hn, wif_ref[0])


def _first_h(x2, nw_all, mod_t, wif, *, group, tiles_per_seq, t, nseq, lseq):
    rows = x2.shape[0]
    return pl.pallas_call(
        functools.partial(_h_kernel, tiles_per_seq=tiles_per_seq, nseq=nseq, lseq=lseq),
        grid=(rows // t,),
        in_specs=[pl.BlockSpec((t, D_MODEL), lambda i: (i, 0)),
                  pl.BlockSpec((1, 1, D_MODEL), lambda i: (0, 0, 0)),
                  _mod_spec(0, 0, group), _mod_spec(0, 1, group),
                  pl.BlockSpec((1, D_MODEL, LANES), lambda i: (0, 0, 0))],
        out_specs=[pl.BlockSpec((t, D_MODEL), lambda i: (i, 0)),
                   pl.BlockSpec((t, LANES), lambda i: (i, 0))],
        out_shape=[jax.ShapeDtypeStruct((rows, D_MODEL), BF),
                   jax.ShapeDtypeStruct((rows, LANES), F32)],
        compiler_params=_params(1), name="first_h",
    )(x2, nw_all, mod_t, mod_t, wif)


def _gate_columns(g, hd, b_i, b_f):
    lane = lax.broadcasted_iota(jnp.int32, g.shape, 1)
    ig = jnp.sum(jnp.where(lane == hd, g, 0.0), axis=1, keepdims=True) + b_i
    fg = jnp.sum(jnp.where(lane == hd + N_HEADS, g, 0.0), axis=1, keepdims=True) + b_f
    return ig, _log_sigmoid(fg)


def _recurrence(q, k, v, ig_c, lf_c, c_old, n_old, m0):
    t = q.shape[0]
    qb = q.astype(BF)
    vb = v.astype(BF)
    row = lax.broadcasted_iota(jnp.int32, (t, t), 0)
    col = lax.broadcasted_iota(jnp.int32, (t, t), 1)
    causal = col <= row
    eye = col == row
    ig_r = jnp.sum(jnp.where(eye, ig_c, 0.0), axis=0, keepdims=True)
    lf_r = jnp.sum(jnp.where(eye, lf_c, 0.0), axis=0, keepdims=True)
    b_c = jnp.sum(jnp.where(causal, lf_r, 0.0), axis=1, keepdims=True)
    b_r = jnp.sum(jnp.where(row <= col, lf_c, 0.0), axis=0, keepdims=True)
    g_r = ig_r - b_r
    g_c = ig_c - b_c
    big_m = jnp.maximum(jnp.max(jnp.where(causal, g_r, NEG), axis=1, keepdims=True), m0)
    w_intra = jnp.exp(jnp.where(causal, g_r - big_m, NEG))
    a_inter = jnp.exp(m0 - big_m)

    s = _dot_nt(qb, k.astype(BF)) * w_intra
    num = a_inter * _dot(qb, c_old.astype(BF)) + _dot(s.astype(BF), vb)
    den = a_inter * jnp.sum(q * n_old, axis=1, keepdims=True) + jnp.sum(s, axis=1, keepdims=True)
    h = num * (1.0 / jnp.maximum(jnp.abs(den), jnp.exp(-(b_c + big_m))))

    m_last = big_m[t - 1:t, :]
    kw = k * jnp.exp(g_c - m_last)
    decay = jnp.exp(m0 - m_last)
    c_new = decay * c_old + _dot_tn(kw.astype(BF), vb)
    n_new = decay * n_old + jnp.sum(kw, axis=0, keepdims=True)
    return h, c_new, n_new, b_c[t - 1:t, :] + m_last


def _head_out(o, z, h, hnw):
    og = jax.nn.sigmoid(o) * h
    mu = jnp.mean(og, axis=1, keepdims=True)
    dev = og - mu
    var = jnp.mean(dev * dev, axis=1, keepdims=True)
    return (_silu(z) * (dev * lax.rsqrt(var + EPS) * hnw)).astype(BF)


def _mlstm_kernel(*refs, layer, chunk, aliased, convert):
    bif_ref, h_ref, g_ref, wq_ref, wk_ref, wv_ref, wo_ref, wz_ref, hnw_ref = refs[:9]
    refs = refs[9 + aliased:]
    if convert:
        fa_ref, fb_ref, a_ref, c_ref, n_ref, m_ref, na_ref, nb_ref = refs
        na_ref[...] = fa_ref[...].astype(BF)
        nb_ref[...] = fb_ref[...].astype(BF)
    else:
        a_ref, c_ref, n_ref, m_ref = refs
    hd = pl.program_id(0)

    @pl.when(pl.program_id(2) == 0)
    def _init():
        c_ref[0, 0, 0] = jnp.zeros((DH, DH), F32)
        n_ref[0, 0] = jnp.zeros((1, DH), F32)
        m_ref[0, 0] = jnp.zeros((1, LANES), F32)

    hb = h_ref[...]
    q = _dot_nt(hb, wq_ref[0])
    k = _dot_nt(hb, wk_ref[0]) * (DH ** -0.5)
    v = _dot_nt(hb, wv_ref[0])
    ig_c, lf_c = _gate_columns(g_ref[...], hd, bif_ref[layer, hd], bif_ref[layer, N_HEADS + hd])
    c, n, m = c_ref[0, 0, 0], n_ref[0, 0], m_ref[0, 0][:, 0:1]
    hs = []
    for j in range(hb.shape[0] // chunk):
        r = slice(j * chunk, (j + 1) * chunk)
        h, c, n, m = _recurrence(q[r], k[r], v[r], ig_c[r], lf_c[r], c, n, m)
        hs.append(h)
    c_ref[0, 0, 0] = c
    n_ref[0, 0] = n
    m_ref[0, 0] = jnp.broadcast_to(m, (1, LANES))
    a_ref[...] = _head_out(_dot_nt(hb, wo_ref[0]), _dot_nt(hb, wz_ref[0]), jnp.concatenate(hs, axis=0), hnw_ref[0])


def _w5_specs(head_of):
    qkvo = [pl.BlockSpec((1, DH, D_MODEL), lambda *g, grp=grp: (0, grp * N_HEADS + head_of(*g), 0))
            for grp in range(4)]
    return qkvo + [pl.BlockSpec((1, DH, D_MODEL), lambda *g: (0, head_of(*g), 0))]


def _state_out(depth, batch, layer, c_stack, index_map, n_inputs, per_step=1):
    specs = [pl.BlockSpec((1, per_step, 1, DH, DH), lambda *g: (layer,) + index_map(*g)),
             pl.BlockSpec((per_step, 1, 1, DH), index_map),
             pl.BlockSpec((per_step, 1, 1, LANES), index_map)]
    shapes = [jax.ShapeDtypeStruct((depth, batch, N_HEADS, DH, DH), F32),
              jax.ShapeDtypeStruct((batch, N_HEADS, 1, DH), F32),
              jax.ShapeDtypeStruct((batch, N_HEADS, 1, LANES), F32)]
    if c_stack is None:
        return specs, shapes, [], [], {}
    return specs, shapes, [pl.BlockSpec(memory_space=pl.ANY)], [c_stack], {n_inputs: 1}


def _mlstm_prompt(h2, gates, wts, b_if, hnw, c_stack, w_in_t, *, depth, layer, batch, seq_rows, t, chunk):
    nck = seq_rows // t
    rows = batch * seq_rows
    steps = N_HEADS * batch * nck
    convert = layer + 1 < depth
    in_specs = ([pl.BlockSpec(memory_space=pltpu.SMEM),
                 pl.BlockSpec((t, D_MODEL), lambda hd, b, c: (b * nck + c, 0)),
                 pl.BlockSpec((t, LANES), lambda hd, b, c: (b * nck + c, 0))]
                + _w5_specs(lambda hd, b, c: hd)
                + [pl.BlockSpec((1, 1, DH), lambda hd, b, c: (layer, 0, hd))])
    args = [b_if, h2, gates, wts["w5a"], wts["w5a"], wts["w5a"], wts["w5a"], wts["w5b"], hnw]
    st_specs, st_shapes, extra_specs, extra_args, aliases = _state_out(
        depth, batch, layer, c_stack, lambda hd, b, c: (b, hd, 0, 0), len(in_specs))
    in_specs += extra_specs
    args += extra_args
    out_specs = [pl.BlockSpec((t, DH), lambda hd, b, c: (b * nck + c, hd))] + st_specs
    out_shape = [jax.ShapeDtypeStruct((rows, D_MODEL), BF)] + st_shapes
    if convert:
        assert N_QKVO % (2 * SUBLANES * steps) == 0 and D_MODEL % (2 * SUBLANES * steps) == 0
        ra, rb = N_QKVO // steps, D_MODEL // steps
        step = lambda hd, b, c: (hd * batch + b) * nck + c
        in_specs += [pl.BlockSpec((1, ra, D_MODEL), lambda *g: (layer + 1, step(*g), 0)),
                     pl.BlockSpec((1, rb, D_MODEL), lambda *g: (layer + 1, N_QKVO // rb + step(*g), 0))]
        args += [w_in_t, w_in_t]
        out_specs += [pl.BlockSpec((1, ra, D_MODEL), lambda *g: (0, step(*g), 0)),
                      pl.BlockSpec((1, rb, D_MODEL), lambda *g: (0, step(*g), 0))]
        out_shape += [jax.ShapeDtypeStruct((1, N_QKVO, D_MODEL), BF),
                      jax.ShapeDtypeStruct((1, D_MODEL, D_MODEL), BF)]
    return pl.pallas_call(
        functools.partial(_mlstm_kernel, layer=layer, chunk=chunk, aliased=c_stack is not None, convert=convert),
        grid=(N_HEADS, batch, nck),
        in_specs=in_specs, out_specs=out_specs, out_shape=out_shape,
        input_output_aliases=aliases,
        compiler_params=_params(3), name="mlstm",
    )(*args)


def _mlstm_seqs_kernel(bif_ref, h_ref, g_ref, wq_ref, wk_ref, wv_ref, wo_ref, wz_ref, hnw_ref,
                       c0_ref, n0_ref, m0_ref, *rest, layer, lseq, per_step):
    a_ref, c_ref, n_ref, m_ref, q_s, k_s, v_s, h_s, ig_s, lf_s = rest[-10:]
    hd = pl.program_id(0)
    b = pl.program_id(1)

    @pl.when(b == 0)
    def _project():
        hb = h_ref[...]
        q_s[...] = _dot_nt(hb, wq_ref[0])
        k_s[...] = _dot_nt(hb, wk_ref[0]) * (DH ** -0.5)
        v_s[...] = _dot_nt(hb, wv_ref[0])
        ig_c, lf_c = _gate_columns(g_ref[...], hd, bif_ref[layer, hd], bif_ref[layer, N_HEADS + hd])
        ig_s[...] = ig_c
        lf_s[...] = lf_c

    for j in range(per_step):
        rows = pl.ds(pl.multiple_of((b * per_step + j) * lseq, lseq), lseq)
        h, c_new, n_new, m_new = _recurrence(q_s[rows, :], k_s[rows, :], v_s[rows, :], ig_s[rows, :],
                                             lf_s[rows, :], c0_ref[0, j, 0], n0_ref[0, j, 0],
                                             m0_ref[0, j, 0][:, 0:1])
        c_ref[0, j, 0] = c_new
        n_ref[j, 0] = n_new
        m_ref[j, 0] = jnp.broadcast_to(m_new, (1, LANES))
        h_s[rows, :] = h

    @pl.when(b == pl.num_programs(1) - 1)
    def _finish():
        hb = h_ref[...]
        a_ref[...] = _head_out(_dot_nt(hb, wo_ref[0]), _dot_nt(hb, wz_ref[0]), h_s[...], hnw_ref[0])


def _mlstm_seqs(h2, gates, wts, b_if, hnw, state_c, state_n, state_m, c_stack, *, depth, layer, batch, lseq,
                per_step):
    rows = batch * lseq
    in_specs = ([pl.BlockSpec(memory_space=pltpu.SMEM),
                 pl.BlockSpec((rows, D_MODEL), lambda hd, b: (0, 0)),
                 pl.BlockSpec((rows, LANES), lambda hd, b: (0, 0))]
                + _w5_specs(lambda hd, b: hd)
                + [pl.BlockSpec((1, 1, DH), lambda hd, b: (layer, 0, hd)),
                   pl.BlockSpec((1, per_step, 1, DH, DH), lambda hd, b: (layer, b, hd, 0, 0)),
                   pl.BlockSpec((1, per_step, 1, 1, DH), lambda hd, b: (layer, b, hd, 0, 0)),
                   pl.BlockSpec((1, per_step, 1, 1, LANES), lambda hd, b: (layer, b, hd, 0, 0))])
    st_specs, st_shapes, extra_specs, extra_args, aliases = _state_out(
        depth, batch, layer, c_stack, lambda hd, b: (b, hd, 0, 0), len(in_specs), per_step)
    return pl.pallas_call(
        functools.partial(_mlstm_seqs_kernel, layer=layer, lseq=lseq, per_step=per_step),
        grid=(N_HEADS, batch // per_step),
        in_specs=in_specs + extra_specs,
        out_specs=[pl.BlockSpec((rows, DH), lambda hd, b: (0, hd))] + st_specs,
        out_shape=[jax.ShapeDtypeStruct((rows, D_MODEL), BF)] + st_shapes,
        input_output_aliases=aliases,
        scratch_shapes=[pltpu.VMEM((rows, DH), F32)] * 4 + [pltpu.VMEM((rows, 1), F32)] * 2,
        compiler_params=_params(2), name="mlstm_seqs",
    )(b_if, h2, gates, wts["w5a"], wts["w5a"], wts["w5a"], wts["w5a"], wts["w5b"], hnw,
      state_c, state_n, state_m, *extra_args)


def _pool_kernel(*refs, lseq, nseq, pos0, first, has_state, convert):
    src_ref, w_ref, wmix_ref, ps_ref = refs[:4]
    refs = refs[4:]
    if first:
        nw_ref, shift_ref, scl_ref, wif_ref = refs[:4]
        refs = refs[4:]
    if has_state:
        hist0_ref, refs = refs[0], refs[1:]
    if convert:
        ra_ref, rb_ref = refs[:2]
        refs = refs[2:]
    bb_ref, sga_ref, sgb_ref, hist_ref = refs[:4]
    uext = refs[-1]
    refs = refs[4:-1]
    if convert:
        refs[-1][0] = _shifted_rows(ra_ref[0], rb_ref[0]).astype(BF)
    i = pl.program_id(1)
    seg = HIST_ROWS + lseq

    @pl.when(i == 0)
    def _first():
        for sq in range(nseq):
            uext[sq * seg:sq * seg + HIST_ROWS, :] = (hist0_ref[0, sq] if has_state
                                                       else jnp.zeros((HIST_ROWS, D_B), F32))

    @pl.when(i > 0)
    def _carry():
        uext[0:HIST_ROWS, :] = uext[lseq:seg, :]

    if first:
        seq = pl.program_id(0)
        hb = _norm_mod(src_ref[...], nw_ref[0], _mod_vec(scl_ref, seq, nseq, lseq),
                       _mod_vec(shift_ref, seq, nseq, lseq)).astype(BF)
        refs[0][...] = hb
        refs[1][...] = _dot(hb, wif_ref[0])
    else:
        hb = src_ref[...]
    u = _dot_nt(hb, w_ref[0, 0:D_B, :])
    for sq in range(nseq):
        uext[sq * seg + HIST_ROWS:(sq + 1) * seg, :] = u[sq * lseq:(sq + 1) * lseq, :]
    zb = _dot_nt(hb, w_ref[0, D_B:2 * D_B, :])
    sga_ref[...] = jax.nn.sigmoid(_dot_nt(hb, w_ref[0, 2 * D_B:2 * D_B + D_MODEL, :]))
    sgb_ref[...] = jax.nn.sigmoid(_dot_nt(hb, w_ref[0, 2 * D_B + D_MODEL:, :]))

    in_seq = lax.broadcasted_iota(jnp.int32, (lseq, 1), 0)
    pos = jnp.concatenate([in_seq] * nseq, axis=0) + (i * lseq + pos0 + 1)
    for g, w in enumerate(POOL_WINDOWS):
        sl = slice(g * POOL_GROUP, (g + 1) * POOL_GROUP)
        acc = uext[:, sl]
        shift = 1
        while shift < w:
            acc = acc + pltpu.roll(acc, shift, axis=0)
            shift *= 2
        win = jnp.concatenate([acc[sq * seg + HIST_ROWS:(sq + 1) * seg, :] for sq in range(nseq)], axis=0)
        pooled = win / jnp.minimum(pos, w).astype(F32)
        dlt = (pooled - u[:, sl]).astype(BF)
        mixed = _dot(dlt, wmix_ref[0, g].astype(BF)) * ps_ref[0][:, sl]
        bb_ref[:, sl] = (_silu(zb[:, sl]) * mixed).astype(BF)

    @pl.when(i == pl.num_programs(1) - 1)
    def _hist():
        for sq in range(nseq):
            hist_ref[sq] = uext[sq * seg + lseq:(sq + 1) * seg, :]


def _pool(src, w_rest, wmix, ps, hist0, first_inputs, w_in_t, *, layer, groups, tiles, lseq, nseq, pos0, convert):
    t = lseq * nseq
    rows = groups * tiles * t
    has_state = hist0 is not None
    first = first_inputs is not None
    row_spec = lambda width: pl.BlockSpec((t, width), lambda b, i: (b * tiles + i, 0))
    in_specs = [row_spec(D_MODEL), _layer_spec(w_rest.shape, 0), _layer_spec(wmix.shape, layer),
                _layer_spec(ps.shape, layer)]
    args = [src, w_rest, wmix, ps]
    if first:
        nw_all, mod_t, wif, group = first_inputs
        in_specs += [pl.BlockSpec((1, 1, D_MODEL), lambda b, i: (layer, 0, 0)),
                     _mod_spec(layer, 0, group), _mod_spec(layer, 1, group),
                     pl.BlockSpec((1, D_MODEL, LANES), lambda b, i: (layer, 0, 0))]
        args += [nw_all, mod_t, mod_t, wif]
    if has_state:
        in_specs.append(pl.BlockSpec((1, nseq, HIST_ROWS, D_B), lambda b, i: (layer, b, 0, 0)))
        args.append(hist0)
    out_specs = [row_spec(D_B), row_spec(D_MODEL), row_spec(D_MODEL),
                 pl.BlockSpec((nseq, HIST_ROWS, D_B), lambda b, i: (b, 0, 0))]
    out_shape = [jax.ShapeDtypeStruct((rows, D_B), BF),
                 jax.ShapeDtypeStruct((rows, D_MODEL), F32),
                 jax.ShapeDtypeStruct((rows, D_MODEL), F32),
                 jax.ShapeDtypeStruct((groups * nseq, HIST_ROWS, D_B), F32)]
    if first:
        out_specs += [row_spec(D_MODEL), row_spec(LANES)]
        out_shape += [jax.ShapeDtypeStruct((rows, D_MODEL), BF), jax.ShapeDtypeStruct((rows, LANES), F32)]
    if convert:
        n_blk = N_REST // LANES
        assert groups * tiles >= n_blk
        blk = lambda b, i: jnp.minimum(b * tiles + i, n_blk - 1)
        in_specs += _shift_specs(layer + 1, LANES, blk)
        args += [w_in_t, w_in_t]
        out_specs.append(pl.BlockSpec((1, LANES, D_MODEL), lambda b, i: (0, blk(b, i), 0)))
        out_shape.append(jax.ShapeDtypeStruct((1, N_REST, D_MODEL), BF))
    return pl.pallas_call(
        functools.partial(_pool_kernel, lseq=lseq, nseq=nseq, pos0=pos0, first=first, has_state=has_state,
                          convert=convert),
        grid=(groups, tiles),
        in_specs=in_specs, out_specs=out_specs, out_shape=out_shape,
        scratch_shapes=[pltpu.VMEM((nseq * (HIST_ROWS + lseq), D_B), F32)],
        compiler_params=_params(2), name="pool",
    )(*args)


def _out_kernel(*refs, last, tiles_per_seq, nseq, lseq, convert):
    x_ref, a_ref, bb_ref, sga_ref, sgb_ref, gate_ref, wa_ref, wb_ref, wo_ref, nw_ref = refs[:10]
    refs = refs[10:]
    if not last:
        shift_ref, scl_ref, wif_ref = refs[:3]
        refs = refs[3:]
    if convert:
        fa_ref, fb_ref, fo_ref = refs[:3]
        refs = refs[3:]
        na_ref, nb_ref, no_ref = refs[-3:]
        na_ref[...] = fa_ref[...].astype(BF)
        nb_ref[...] = fb_ref[...].astype(BF)
        no_ref[...] = fo_ref[...].astype(BF)
    seq = pl.program_id(0) // tiles_per_seq
    br_a = _dot(a_ref[...], wa_ref[0])
    br_b = _dot(bb_ref[...], wb_ref[0])
    merged = (sga_ref[...] * br_a + sgb_ref[...] * br_b).astype(BF)
    y = x_ref[...] + _mod_vec(gate_ref, seq, nseq, lseq) * _dot(merged, wo_ref[0])
    if last:
        r = lax.rsqrt(jnp.mean(y * y, axis=-1, keepdims=True) + EPS)
        refs[0][...] = y * r * nw_ref[0]
    else:
        y_ref, hn_ref, g_ref = refs[:3]
        y_ref[...] = y
        hn = _norm_mod(y, nw_ref[0], _mod_vec(scl_ref, seq, nseq, lseq),
                       _mod_vec(shift_ref, seq, nseq, lseq)).astype(BF)
        hn_ref[...] = hn
        g_ref[...] = _dot(hn, wif_ref[0])


def _out(x2, a2, bb2, sga, sgb, mod_t, wts, nw_all, wif, f32_weights, *, layer, group, tiles_per_seq, t, nseq,
         lseq, last, convert):
    rows = x2.shape[0]
    steps = rows // t
    row_spec = lambda width: pl.BlockSpec((t, width), lambda i: (i, 0))
    in_specs = [row_spec(D_MODEL), row_spec(D_MODEL), row_spec(D_B), row_spec(D_MODEL), row_spec(D_MODEL),
                _mod_spec(layer, 2, group),
                _layer_spec(wts["wa"].shape, 0), _layer_spec(wts["wb"].shape, 0), _layer_spec(wts["wo"].shape, 0),
                pl.BlockSpec((1, 1, D_MODEL), lambda i: (layer + 1, 0, 0))]
    args = [x2, a2, bb2, sga, sgb, mod_t, wts["wa"], wts["wb"], wts["wo"], nw_all]
    if last:
        out_specs = [row_spec(D_MODEL)]
        out_shape = [jax.ShapeDtypeStruct((rows, D_MODEL), F32)]
    else:
        in_specs += [_mod_spec(layer + 1, 0, group), _mod_spec(layer + 1, 1, group),
                     pl.BlockSpec((1, D_MODEL, LANES), lambda i: (layer + 1, 0, 0))]
        args += [mod_t, mod_t, wif]
        out_specs = [row_spec(D_MODEL), row_spec(D_MODEL), row_spec(LANES)]
        out_shape = [jax.ShapeDtypeStruct((rows, D_MODEL), F32),
                     jax.ShapeDtypeStruct((rows, D_MODEL), BF),
                     jax.ShapeDtypeStruct((rows, LANES), F32)]
    if convert:
        for w in f32_weights:
            _, kdim, n = w.shape
            assert kdim % (2 * SUBLANES * steps) == 0
            in_specs.append(pl.BlockSpec((1, kdim // steps, n), lambda i: (layer + 1, i, 0)))
            args.append(w)
            out_specs.append(pl.BlockSpec((1, kdim // steps, n), lambda i: (0, i, 0)))
            out_shape.append(jax.ShapeDtypeStruct((1, kdim, n), BF))
    return pl.pallas_call(
        functools.partial(_out_kernel, last=last, tiles_per_seq=tiles_per_seq, nseq=nseq, lseq=lseq,
                          convert=convert),
        grid=(steps,),
        in_specs=in_specs, out_specs=out_specs, out_shape=out_shape,
        compiler_params=_params(1), name="out",
    )(*args)


def kernel(x_prompt, x_sample, c_prompt, c_sample, state_C, state_n, state_m, state_pool,
           norm_w, w_ada, b_ada, w_in, b_if, head_norm_w, w_pool_mix, pool_scale,
           w_branch_a, w_branch_b, w_out, final_norm_w):
    depth = w_in.shape[0]
    bp, lp, _ = x_prompt.shape
    bs, ls, _ = x_sample.shape
    tp = 256
    t_mlstm = 4 * tp
    assert bp <= SUBLANES and bs == SUBLANES and lp % t_mlstm == 0 and ls >= POOL_HIST and ls % SUBLANES == 0

    w_in_t = jnp.swapaxes(w_in, 1, 2)
    w_rest, wif = _cast_pool_side(w_in_t)
    wts = dict(w5a=_cast_first_layer(w_in_t, 1024, 0, N_QKVO),
               w5b=_cast_first_layer(w_in_t, 1024, N_QKVO, D_MODEL), w_rest=w_rest,
               wa=_cast_first_layer(w_branch_a, 512), wb=_cast_first_layer(w_branch_b, 512),
               wo=_cast_first_layer(w_out, 512))

    c_all = jnp.zeros((MOD_ROWS, D_MODEL), F32).at[:bp].set(c_prompt).at[SUBLANES:].set(c_sample)
    mod = _modulation(c_all, w_ada, b_ada)
    mod_t = mod.reshape(depth, MOD_ROWS, 3, D_MODEL).transpose(0, 2, 1, 3)

    nw_all = jnp.concatenate([norm_w, final_norm_w[None, :]], axis=0).reshape(depth + 1, 1, D_MODEL)
    hnw = head_norm_w.reshape(depth, 1, D_MODEL)
    ps = pool_scale.reshape(depth, 1, D_B)
    st_n = state_n.reshape(depth, bs, N_HEADS, 1, DH)
    st_m = jnp.broadcast_to(state_m[..., None, None], (depth, bs, N_HEADS, 1, LANES))
    hist0 = jnp.pad(state_pool, ((0, 0), (0, 0), (HIST_ROWS - POOL_HIST, 0), (0, 0)))

    xp = x_prompt.reshape(bp * lp, D_MODEL)
    xs = x_sample.reshape(bs * ls, D_MODEL)
    prompt = dict(group=0, tiles_per_seq=lp // tp, t=tp, nseq=1, lseq=tp)
    sample = dict(group=1, tiles_per_seq=1, t=bs * ls, nseq=bs, lseq=ls)
    hs, gs = _first_h(xs, nw_all, mod_t, wif, **sample)
    hp = gp = None

    outs = {k: [] for k in ("np", "mp", "pp", "ns", "ms", "ps")}
    c1 = c2 = None
    for l in range(depth):
        last = l == depth - 1
        nxt = {}
        res = _pool(xp if l == 0 else hp, wts["w_rest"], w_pool_mix, ps, None,
                    (nw_all, mod_t, wif, 0) if l == 0 else None, w_in_t,
                    layer=l, groups=bp, tiles=lp // tp, lseq=tp, nseq=1, pos0=0, convert=not last)
        bb_p, sga_p, sgb_p, hist_p = res[:4]
        if l == 0:
            hp, gp = res[4:6]
        if not last:
            nxt["w_rest"] = res[-1]
        res = _mlstm_prompt(hp, gp, wts, b_if, hnw, c1, w_in_t, depth=depth, layer=l, batch=bp,
                            seq_rows=lp, t=t_mlstm, chunk=tp)
        a_p, c1, n1, m1 = res[:4]
        if not last:
            nxt["w5a"], nxt["w5b"] = res[4:]
        res = _out(xp, a_p, bb_p, sga_p, sgb_p, mod_t, wts, nw_all, wif, (w_branch_a, w_branch_b, w_out),
                   layer=l, last=last, convert=not last, **prompt)
        if last:
            (yp,) = res
        else:
            xp, hp, gp, nxt["wa"], nxt["wb"], nxt["wo"] = res
        outs["np"].append(n1[:, :, 0, :])
        outs["mp"].append(m1[:, :, 0, 0])
        outs["pp"].append(hist_p[:, HIST_ROWS - POOL_HIST:, :])

        a_s, c2, n2, m2 = _mlstm_seqs(hs, gs, wts, b_if, hnw, state_C, st_n, st_m, c2, depth=depth, layer=l,
                                      batch=bs, lseq=ls, per_step=4)
        bb_s, sga_s, sgb_s, hist_s = _pool(hs, wts["w_rest"], w_pool_mix, ps, hist0, None, None, layer=l,
                                           groups=1, tiles=1, lseq=ls, nseq=bs, pos0=PAST_LEN, convert=False)
        res = _out(xs, a_s, bb_s, sga_s, sgb_s, mod_t, wts, nw_all, wif, None, layer=l, last=last,
                   convert=False, **sample)
        if last:
            (ys,) = res
        else:
            xs, hs, gs = res
        outs["ns"].append(n2[:, :, 0, :])
        outs["ms"].append(m2[:, :, 0, 0])
        outs["ps"].append(hist_s[:, HIST_ROWS - POOL_HIST:, :])
        wts = nxt

    st = {k: jnp.stack(v) for k, v in outs.items()}
    return (yp.reshape(bp, lp, D_MODEL), ys.reshape(bs, ls, D_MODEL),
            c1, st["np"], st["mp"], st["pp"],
            c2, st["ns"], st["ms"], st["ps"])
```

```python
import functools

import jax
import jax.numpy as jnp
from jax import lax
from jax.experimental import pallas as pl
from jax.experimental.pallas import tpu as pltpu

D_MODEL = 2048
N_HEADS = 4
DH = D_MODEL // N_HEADS
D_B = D_MODEL // 2
POOL_WINDOWS = (2, 4, 8, 16)
POOL_GROUP = D_B // len(POOL_WINDOWS)
POOL_HIST = 15
HIST_ROWS = 16
PAST_LEN = 2048
EPS = 1e-6
NEG = -1e30
N_QKVO = 4 * D_MODEL
N_QKVOZ = 5 * D_MODEL
N_IF = 2 * N_HEADS
N_REST = 2 * D_B + 2 * D_MODEL

LANES = 128
SUBLANES = 8
VMEM_LIMIT = 60 * 1024 * 1024
MOD_ROWS = 2 * SUBLANES

BF = jnp.bfloat16
F32 = jnp.float32

_dot = functools.partial(jnp.dot, preferred_element_type=F32)


def _dot_nt(a, b):
    return lax.dot_general(a, b, (((1,), (1,)), ((), ())), preferred_element_type=F32)


def _dot_tn(a, b):
    return lax.dot_general(a, b, (((0,), (0,)), ((), ())), preferred_element_type=F32)


def _params(n_axes):
    return pltpu.CompilerParams(dimension_semantics=("arbitrary",) * n_axes,
                                vmem_limit_bytes=VMEM_LIMIT)


def _layer_spec(shape, layer):
    nd = len(shape)
    return pl.BlockSpec((1,) + tuple(shape[1:]), lambda *_: (layer,) + (0,) * (nd - 1),
                        pipeline_mode=pl.Buffered(1))


def _silu(x):
    return x * jax.nn.sigmoid(x)


def _log_sigmoid(x):
    return jnp.minimum(x, 0.0) - jnp.log(1.0 + jnp.exp(-jnp.abs(x)))


def _shifted_rows(a_rows, next_rows):
    return jnp.concatenate([a_rows[N_IF:, :], next_rows], axis=0)


def _shift_specs(layer, nb, block_of_step):
    first = N_QKVOZ // nb
    return [pl.BlockSpec((1, nb, D_MODEL), lambda *g: (layer, first + block_of_step(*g), 0)),
            pl.BlockSpec((1, N_IF, D_MODEL),
                         lambda *g: (layer, (first + block_of_step(*g) + 1) * (nb // N_IF), 0))]


def _cast_shift_kernel(a_ref, b_ref, o_ref):
    o_ref[0] = _shifted_rows(a_ref[0], b_ref[0]).astype(BF)


def _cast_gate_kernel(w_ref, o_ref):
    rows = jnp.concatenate([w_ref[0], jnp.zeros((LANES - N_IF, D_MODEL), F32)], axis=0)
    o_ref[0] = rows.T.astype(BF)


def _cast_pool_side(w_in_t):
    depth = w_in_t.shape[0]
    nb = 1024
    w_rest = pl.pallas_call(
        _cast_shift_kernel, grid=(N_REST // nb,),
        in_specs=_shift_specs(0, nb, lambda j: j),
        out_specs=pl.BlockSpec((1, nb, D_MODEL), lambda j: (0, j, 0)),
        out_shape=jax.ShapeDtypeStruct((1, N_REST, D_MODEL), BF),
        compiler_params=_params(1), name="cast_rest")(w_in_t, w_in_t)
    wif = pl.pallas_call(
        _cast_gate_kernel, grid=(depth,),
        in_specs=[pl.BlockSpec((1, N_IF, D_MODEL), lambda l: (l, N_QKVOZ // N_IF, 0))],
        out_specs=pl.BlockSpec((1, D_MODEL, LANES), lambda l: (l, 0, 0)),
        out_shape=jax.ShapeDtypeStruct((depth, D_MODEL, LANES), BF),
        compiler_params=_params(1), name="cast_gates")(w_in_t)
    return w_rest, wif


def _mod_kernel(c_ref, w_ref, b_ref, o_ref):
    @pl.when(pl.program_id(1) == 0)
    def _init():
        o_ref[0] = jnp.broadcast_to(b_ref[0], o_ref.shape[1:])

    o_ref[0] += _dot(_silu(c_ref[...]).astype(BF), w_ref[0].astype(BF))


def _modulation(c_all, w_ada, b_ada):
    depth, kdim, n = w_ada.shape
    tk = 512
    return pl.pallas_call(
        _mod_kernel,
        grid=(depth, kdim // tk),
        in_specs=[pl.BlockSpec((MOD_ROWS, tk), lambda l, k: (0, k)),
                  pl.BlockSpec((1, tk, n), lambda l, k: (l, k, 0)),
                  pl.BlockSpec((1, 1, n), lambda l, k: (l, 0, 0))],
        out_specs=pl.BlockSpec((1, MOD_ROWS, n), lambda l, k: (l, 0, 0)),
        out_shape=jax.ShapeDtypeStruct((depth, MOD_ROWS, n), F32),
        compiler_params=_params(2), name="modulation",
    )(c_all, w_ada, b_ada.reshape(depth, 1, n))


def _mod_vec(ref, seq, nseq, lseq):
    if nseq == 1:
        return ref[0, 0, pl.ds(seq, 1), :]
    blk = ref[0, 0]
    return jnp.concatenate([jnp.broadcast_to(blk[s:s + 1, :], (lseq, D_MODEL)) for s in range(nseq)], axis=0)


def _mod_spec(layer, kind, group):
    return pl.BlockSpec((1, 1, SUBLANES, D_MODEL), lambda *_: (layer, kind, group, 0))


def _norm_mod(y, nw, scl, shift):
    r = lax.rsqrt(jnp.mean(y * y, axis=-1, keepdims=True) + EPS)
    return (y * r * nw) * (1.0 + scl) + shift


def _h_kernel(x_ref, nw_ref, shift_ref, scl_ref, wif_ref, h_ref, g_ref, *, tiles_per_seq, nseq, lseq):
    seq = pl.program_id(0) // tiles_per_seq
    hn = _norm_mod(x_ref[...], nw_ref[0], _mod_vec(scl_ref, seq, nseq, lseq),
                   _mod_vec(shift_ref, seq, nseq, lseq)).astype(BF)
    h_ref[...] = hn
    g_ref[...] = _dot(hn, wif_ref[0])


def _first_h(x2, nw_all, mod_t, wif, *, group, tiles_per_seq, t, nseq, lseq):
    rows = x2.shape[0]
    return pl.pallas_call(
        functools.partial(_h_kernel, tiles_per_seq=tiles_per_seq, nseq=nseq, lseq=lseq),
        grid=(rows // t,),
        in_specs=[pl.BlockSpec((t, D_MODEL), lambda i: (i, 0)),
                  pl.BlockSpec((1, 1, D_MODEL), lambda i: (0, 0, 0)),
                  _mod_spec(0, 0, group), _mod_spec(0, 1, group),
                  pl.BlockSpec((1, D_MODEL, LANES), lambda i: (0, 0, 0))],
        out_specs=[pl.BlockSpec((t, D_MODEL), lambda i: (i, 0)),
                   pl.BlockSpec((t, LANES), lambda i: (i, 0))],
        out_shape=[jax.ShapeDtypeStruct((rows, D_MODEL), BF),
                   jax.ShapeDtypeStruct((rows, LANES), F32)],
        compiler_params=_params(1), name="first_h",
    )(x2, nw_all, mod_t, mod_t, wif)


def _gate_columns(g, hd, b_i, b_f):
    lane = lax.broadcasted_iota(jnp.int32, g.shape, 1)
    ig = jnp.sum(jnp.where(lane == hd, g, 0.0), axis=1, keepdims=True) + b_i
    fg = jnp.sum(jnp.where(lane == hd + N_HEADS, g, 0.0), axis=1, keepdims=True) + b_f
    return ig, _log_sigmoid(fg)


def _recurrence(q, k, v, ig_c, lf_c, c_old, n_old, m0):
    t = q.shape[0]
    qb = q.astype(BF)
    vb = v.astype(BF)
    row = lax.broadcasted_iota(jnp.int32, (t, t), 0)
    col = lax.broadcasted_iota(jnp.int32, (t, t), 1)
    causal = col <= row
    eye = col == row
    ig_r = jnp.sum(jnp.where(eye, ig_c, 0.0), axis=0, keepdims=True)
    lf_r = jnp.sum(jnp.where(eye, lf_c, 0.0), axis=0, keepdims=True)
    b_c = jnp.sum(jnp.where(causal, lf_r, 0.0), axis=1, keepdims=True)
    b_r = jnp.sum(jnp.where(row <= col, lf_c, 0.0), axis=0, keepdims=True)
    g_r = ig_r - b_r
    g_c = ig_c - b_c
    big_m = jnp.maximum(jnp.max(jnp.where(causal, g_r, NEG), axis=1, keepdims=True), m0)
    w_intra = jnp.exp(jnp.where(causal, g_r - big_m, NEG))
    a_inter = jnp.exp(m0 - big_m)

    s = _dot_nt(qb, k.astype(BF)) * w_intra
    num = a_inter * _dot(qb, c_old.astype(BF)) + _dot(s.astype(BF), vb)
    den = a_inter * jnp.sum(q * n_old, axis=1, keepdims=True) + jnp.sum(s, axis=1, keepdims=True)
    h = num * (1.0 / jnp.maximum(jnp.abs(den), jnp.exp(-(b_c + big_m))))

    m_last = big_m[t - 1:t, :]
    kw = k * jnp.exp(g_c - m_last)
    decay = jnp.exp(m0 - m_last)
    c_new = decay * c_old + _dot_tn(kw.astype(BF), vb)
    n_new = decay * n_old + jnp.sum(kw, axis=0, keepdims=True)
    return h, c_new, n_new, b_c[t - 1:t, :] + m_last


def _head_out(o, z, h, hnw):
    og = jax.nn.sigmoid(o) * h
    mu = jnp.mean(og, axis=1, keepdims=True)
    dev = og - mu
    var = jnp.mean(dev * dev, axis=1, keepdims=True)
    return (_silu(z) * (dev * lax.rsqrt(var + EPS) * hnw)).astype(BF)


def _mlstm_kernel(*refs, layer, chunk, aliased, convert):
    bif_ref, h_ref, g_ref, wq_ref, wk_ref, wv_ref, wo_ref, wz_ref, hnw_ref = refs[:9]
    refs = refs[9 + aliased:]
    if convert:
        fa_ref, fb_ref, a_ref, c_ref, n_ref, m_ref, na_ref, nb_ref = refs
        na_ref[...] = fa_ref[...].astype(BF)
        nb_ref[...] = fb_ref[...].astype(BF)
    else:
        a_ref, c_ref, n_ref, m_ref = refs
    hd = pl.program_id(0)

    @pl.when(pl.program_id(2) == 0)
    def _init():
        c_ref[0, 0, 0] = jnp.zeros((DH, DH), F32)
        n_ref[0, 0] = jnp.zeros((1, DH), F32)
        m_ref[0, 0] = jnp.zeros((1, LANES), F32)

    hb = h_ref[...]
    q = _dot_nt(hb, wq_ref[0])
    k = _dot_nt(hb, wk_ref[0]) * (DH ** -0.5)
    v = _dot_nt(hb, wv_ref[0])
    ig_c, lf_c = _gate_columns(g_ref[...], hd, bif_ref[layer, hd], bif_ref[layer, N_HEADS + hd])
    c, n, m = c_ref[0, 0, 0], n_ref[0, 0], m_ref[0, 0][:, 0:1]
    hs = []
    for j in range(hb.shape[0] // chunk):
        r = slice(j * chunk, (j + 1) * chunk)
        h, c, n, m = _recurrence(q[r], k[r], v[r], ig_c[r], lf_c[r], c, n, m)
        hs.append(h)
    c_ref[0, 0, 0] = c
    n_ref[0, 0] = n
    m_ref[0, 0] = jnp.broadcast_to(m, (1, LANES))
    a_ref[...] = _head_out(_dot_nt(hb, wo_ref[0]), _dot_nt(hb, wz_ref[0]), jnp.concatenate(hs, axis=0), hnw_ref[0])


def _w5_specs(head_of):
    qkvo = [pl.BlockSpec((1, DH, D_MODEL), lambda *g, grp=grp: (0, grp * N_HEADS + head_of(*g), 0))
            for grp in range(4)]
    return qkvo + [pl.BlockSpec((1, DH, D_MODEL), lambda *g: (0, head_of(*g), 0))]


def _state_out(depth, batch, layer, c_stack, index_map, n_inputs, per_step=1):
    specs = [pl.BlockSpec((1, per_step, 1, DH, DH), lambda *g: (layer,) + index_map(*g)),
             pl.BlockSpec((per_step, 1, 1, DH), index_map),
             pl.BlockSpec((per_step, 1, 1, LANES), index_map)]
    shapes = [jax.ShapeDtypeStruct((depth, batch, N_HEADS, DH, DH), F32),
              jax.ShapeDtypeStruct((batch, N_HEADS, 1, DH), F32),
              jax.ShapeDtypeStruct((batch, N_HEADS, 1, LANES), F32)]
    if c_stack is None:
        return specs, shapes, [], [], {}
    return specs, shapes, [pl.BlockSpec(memory_space=pl.ANY)], [c_stack], {n_inputs: 1}


def _mlstm_prompt(h2, gates, wts, b_if, hnw, c_stack, w_in_t, *, depth, layer, batch, seq_rows, t, chunk):
    nck = seq_rows // t
    rows = batch * seq_rows
    steps = N_HEADS * batch * nck
    convert = layer + 1 < depth
    in_specs = ([pl.BlockSpec(memory_space=pltpu.SMEM),
                 pl.BlockSpec((t, D_MODEL), lambda hd, b, c: (b * nck + c, 0)),
                 pl.BlockSpec((t, LANES), lambda hd, b, c: (b * nck + c, 0))]
                + _w5_specs(lambda hd, b, c: hd)
                + [pl.BlockSpec((1, 1, DH), lambda hd, b, c: (layer, 0, hd))])
    args = [b_if, h2, gates, wts["w5a"], wts["w5a"], wts["w5a"], wts["w5a"], wts["w5b"], hnw]
    st_specs, st_shapes, extra_specs, extra_args, aliases = _state_out(
        depth, batch, layer, c_stack, lambda hd, b, c: (b, hd, 0, 0), len(in_specs))
    in_specs += extra_specs
    args += extra_args
    out_specs = [pl.BlockSpec((t, DH), lambda hd, b, c: (b * nck + c, hd))] + st_specs
    out_shape = [jax.ShapeDtypeStruct((rows, D_MODEL), BF)] + st_shapes
    if convert:
        assert N_QKVO % (2 * SUBLANES * steps) == 0 and D_MODEL % (2 * SUBLANES * steps) == 0
        ra, rb = N_QKVO // steps, D_MODEL // steps
        step = lambda hd, b, c: (hd * batch + b) * nck + c
        in_specs += [pl.BlockSpec((1, ra, D_MODEL), lambda *g: (layer + 1, step(*g), 0)),
                     pl.BlockSpec((1, rb, D_MODEL), lambda *g: (layer + 1, N_QKVO // rb + step(*g), 0))]
        args += [w_in_t, w_in_t]
        out_specs += [pl.BlockSpec((1, ra, D_MODEL), lambda *g: (0, step(*g), 0)),
                      pl.BlockSpec((1, rb, D_MODEL), lambda *g: (0, step(*g), 0))]
        out_shape += [jax.ShapeDtypeStruct((1, N_QKVO, D_MODEL), BF),
                      jax.ShapeDtypeStruct((1, D_MODEL, D_MODEL), BF)]
    return pl.pallas_call(
        functools.partial(_mlstm_kernel, layer=layer, chunk=chunk, aliased=c_stack is not None, convert=convert),
        grid=(N_HEADS, batch, nck),
        in_specs=in_specs, out_specs=out_specs, out_shape=out_shape,
        input_output_aliases=aliases,
        compiler_params=_params(3), name="mlstm",
    )(*args)


def _mlstm_seqs_kernel(bif_ref, h_ref, g_ref, wq_ref, wk_ref, wv_ref, wo_ref, wz_ref, hnw_ref,
                       c0_ref, n0_ref, m0_ref, *rest, layer, lseq, per_step):
    a_ref, c_ref, n_ref, m_ref, q_s, k_s, v_s, h_s, ig_s, lf_s = rest[-10:]
    hd = pl.program_id(0)
    b = pl.program_id(1)

    @pl.when(b == 0)
    def _project():
        hb = h_ref[...]
        q_s[...] = _dot_nt(hb, wq_ref[0])
        k_s[...] = _dot_nt(hb, wk_ref[0]) * (DH ** -0.5)
        v_s[...] = _dot_nt(hb, wv_ref[0])
        ig_c, lf_c = _gate_columns(g_ref[...], hd, bif_ref[layer, hd], bif_ref[layer, N_HEADS + hd])
        ig_s[...] = ig_c
        lf_s[...] = lf_c

    for j in range(per_step):
        rows = pl.ds(pl.multiple_of((b * per_step + j) * lseq, lseq), lseq)
        h, c_new, n_new, m_new = _recurrence(q_s[rows, :], k_s[rows, :], v_s[rows, :], ig_s[rows, :],
                                             lf_s[rows, :], c0_ref[0, j, 0], n0_ref[0, j, 0],
                                             m0_ref[0, j, 0][:, 0:1])
        c_ref[0, j, 0] = c_new
        n_ref[j, 0] = n_new
        m_ref[j, 0] = jnp.broadcast_to(m_new, (1, LANES))
        h_s[rows, :] = h

    @pl.when(b == pl.num_programs(1) - 1)
    def _finish():
        hb = h_ref[...]
        a_ref[...] = _head_out(_dot_nt(hb, wo_ref[0]), _dot_nt(hb, wz_ref[0]), h_s[...], hnw_ref[0])


def _mlstm_seqs(h2, gates, wts, b_if, hnw, state_c, state_n, state_m, c_stack, *, depth, layer, batch, lseq,
                per_step):
    rows = batch * lseq
    in_specs = ([pl.BlockSpec(memory_space=pltpu.SMEM),
                 pl.BlockSpec((rows, D_MODEL), lambda hd, b: (0, 0)),
                 pl.BlockSpec((rows, LANES), lambda hd, b: (0, 0))]
                + _w5_specs(lambda hd, b: hd)
                + [pl.BlockSpec((1, 1, DH), lambda hd, b: (layer, 0, hd)),
                   pl.BlockSpec((1, per_step, 1, DH, DH), lambda hd, b: (layer, b, hd, 0, 0)),
                   pl.BlockSpec((1, per_step, 1, 1, DH), lambda hd, b: (layer, b, hd, 0, 0)),
                   pl.BlockSpec((1, per_step, 1, 1, LANES), lambda hd, b: (layer, b, hd, 0, 0))])
    st_specs, st_shapes, extra_specs, extra_args, aliases = _state_out(
        depth, batch, layer, c_stack, lambda hd, b: (b, hd, 0, 0), len(in_specs), per_step)
    return pl.pallas_call(
        functools.partial(_mlstm_seqs_kernel, layer=layer, lseq=lseq, per_step=per_step),
        grid=(N_HEADS, batch // per_step),
        in_specs=in_specs + extra_specs,
        out_specs=[pl.BlockSpec((rows, DH), lambda hd, b: (0, hd))] + st_specs,
        out_shape=[jax.ShapeDtypeStruct((rows, D_MODEL), BF)] + st_shapes,
        input_output_aliases=aliases,
        scratch_shapes=[pltpu.VMEM((rows, DH), F32)] * 4 + [pltpu.VMEM((rows, 1), F32)] * 2,
        compiler_params=_params(2), name="mlstm_seqs",
    )(b_if, h2, gates, wts["w5a"], wts["w5a"], wts["w5a"], wts["w5a"], wts["w5b"], hnw,
      state_c, state_n, state_m, *extra_args)


def _pool_kernel(*refs, lseq, nseq, pos0, first, has_state, convert, n_casts):
    src_ref, w_ref, wmix_ref, ps_ref = refs[:4]
    refs = refs[4:]
    if first:
        nw_ref, shift_ref, scl_ref, wif_ref = refs[:4]
        refs = refs[4:]
    if has_state:
        hist0_ref, refs = refs[0], refs[1:]
    if convert:
        ra_ref, rb_ref = refs[:2]
        refs = refs[2:]
    cast_src, refs = refs[:n_casts], refs[n_casts:]
    bb_ref, sga_ref, sgb_ref, hist_ref = refs[:4]
    uext = refs[-1]
    refs = refs[4:-1]
    for src, dst in zip(cast_src, refs[len(refs) - n_casts:]):
        dst[...] = src[...].astype(BF)
    refs = refs[:len(refs) - n_casts]
    if convert:
        refs[-1][0] = _shifted_rows(ra_ref[0], rb_ref[0]).astype(BF)
    i = pl.program_id(1)
    seg = HIST_ROWS + lseq

    @pl.when(i == 0)
    def _first():
        for sq in range(nseq):
            uext[sq * seg:sq * seg + HIST_ROWS, :] = (hist0_ref[0, sq] if has_state
                                                       else jnp.zeros((HIST_ROWS, D_B), F32))

    @pl.when(i > 0)
    def _carry():
        uext[0:HIST_ROWS, :] = uext[lseq:seg, :]

    if first:
        seq = pl.program_id(0)
        hb = _norm_mod(src_ref[...], nw_ref[0], _mod_vec(scl_ref, seq, nseq, lseq),
                       _mod_vec(shift_ref, seq, nseq, lseq)).astype(BF)
        refs[0][...] = hb
        refs[1][...] = _dot(hb, wif_ref[0])
    else:
        hb = src_ref[...]
    u = _dot_nt(hb, w_ref[0, 0:D_B, :])
    for sq in range(nseq):
        uext[sq * seg + HIST_ROWS:(sq + 1) * seg, :] = u[sq * lseq:(sq + 1) * lseq, :]
    zb = _dot_nt(hb, w_ref[0, D_B:2 * D_B, :])
    sga_ref[...] = jax.nn.sigmoid(_dot_nt(hb, w_ref[0, 2 * D_B:2 * D_B + D_MODEL, :]))
    sgb_ref[...] = jax.nn.sigmoid(_dot_nt(hb, w_ref[0, 2 * D_B + D_MODEL:, :]))

    in_seq = lax.broadcasted_iota(jnp.int32, (lseq, 1), 0)
    pos = jnp.concatenate([in_seq] * nseq, axis=0) + (i * lseq + pos0 + 1)
    for g, w in enumerate(POOL_WINDOWS):
        sl = slice(g * POOL_GROUP, (g + 1) * POOL_GROUP)
        acc = uext[:, sl]
        shift = 1
        while shift < w:
            acc = acc + pltpu.roll(acc, shift, axis=0)
            shift *= 2
        win = jnp.concatenate([acc[sq * seg + HIST_ROWS:(sq + 1) * seg, :] for sq in range(nseq)], axis=0)
        pooled = win / jnp.minimum(pos, w).astype(F32)
        dlt = (pooled - u[:, sl]).astype(BF)
        mixed = _dot(dlt, wmix_ref[0, g].astype(BF)) * ps_ref[0][:, sl]
        bb_ref[:, sl] = (_silu(zb[:, sl]) * mixed).astype(BF)

    @pl.when(i == pl.num_programs(1) - 1)
    def _hist():
        for sq in range(nseq):
            hist_ref[sq] = uext[sq * seg + lseq:(sq + 1) * seg, :]


def _pool(src, w_rest, wmix, ps, hist0, first_inputs, w_in_t, *, layer, groups, tiles, lseq, nseq, pos0, convert,
          side_casts=()):
    t = lseq * nseq
    rows = groups * tiles * t
    has_state = hist0 is not None
    first = first_inputs is not None
    row_spec = lambda width: pl.BlockSpec((t, width), lambda b, i: (b * tiles + i, 0))
    in_specs = [row_spec(D_MODEL), _layer_spec(w_rest.shape, 0), _layer_spec(wmix.shape, layer),
                _layer_spec(ps.shape, layer)]
    args = [src, w_rest, wmix, ps]
    if first:
        nw_all, mod_t, wif, group = first_inputs
        in_specs += [pl.BlockSpec((1, 1, D_MODEL), lambda b, i: (layer, 0, 0)),
                     _mod_spec(layer, 0, group), _mod_spec(layer, 1, group),
                     pl.BlockSpec((1, D_MODEL, LANES), lambda b, i: (layer, 0, 0))]
        args += [nw_all, mod_t, mod_t, wif]
    if has_state:
        in_specs.append(pl.BlockSpec((1, nseq, HIST_ROWS, D_B), lambda b, i: (layer, b, 0, 0)))
        args.append(hist0)
    out_specs = [row_spec(D_B), row_spec(D_MODEL), row_spec(D_MODEL),
                 pl.BlockSpec((nseq, HIST_ROWS, D_B), lambda b, i: (b, 0, 0))]
    out_shape = [jax.ShapeDtypeStruct((rows, D_B), BF),
                 jax.ShapeDtypeStruct((rows, D_MODEL), F32),
                 jax.ShapeDtypeStruct((rows, D_MODEL), F32),
                 jax.ShapeDtypeStruct((groups * nseq, HIST_ROWS, D_B), F32)]
    if first:
        out_specs += [row_spec(D_MODEL), row_spec(LANES)]
        out_shape += [jax.ShapeDtypeStruct((rows, D_MODEL), BF), jax.ShapeDtypeStruct((rows, LANES), F32)]
    if convert:
        n_blk = N_REST // LANES
        assert groups * tiles >= n_blk
        blk = lambda b, i: jnp.minimum(b * tiles + i, n_blk - 1)
        in_specs += _shift_specs(layer + 1, LANES, blk)
        args += [w_in_t, w_in_t]
        out_specs.append(pl.BlockSpec((1, LANES, D_MODEL), lambda b, i: (0, blk(b, i), 0)))
        out_shape.append(jax.ShapeDtypeStruct((1, N_REST, D_MODEL), BF))
    steps = groups * tiles
    cast_in_specs = []
    for w, first_row, nrows in side_casts:
        assert nrows % (2 * SUBLANES * steps) == 0 and first_row % (nrows // steps) == 0
        rps, ncols = nrows // steps, w.shape[2]
        cast_in_specs.append(pl.BlockSpec((1, rps, ncols),
                                          lambda b, i, base=first_row // rps: (layer, base + b * tiles + i, 0)))
        out_specs.append(pl.BlockSpec((1, rps, ncols), lambda b, i: (0, b * tiles + i, 0)))
        out_shape.append(jax.ShapeDtypeStruct((1, nrows, ncols), BF))
    in_specs += cast_in_specs
    args += [w for w, _, _ in side_casts]
    return pl.pallas_call(
        functools.partial(_pool_kernel, lseq=lseq, nseq=nseq, pos0=pos0, first=first, has_state=has_state,
                          convert=convert, n_casts=len(side_casts)),
        grid=(groups, tiles),
        in_specs=in_specs, out_specs=out_specs, out_shape=out_shape,
        scratch_shapes=[pltpu.VMEM((nseq * (HIST_ROWS + lseq), D_B), F32)],
        compiler_params=_params(2), name="pool",
    )(*args)


def _out_kernel(*refs, last, tiles_per_seq, nseq, lseq, convert):
    x_ref, a_ref, bb_ref, sga_ref, sgb_ref, gate_ref, wa_ref, wb_ref, wo_ref, nw_ref = refs[:10]
    refs = refs[10:]
    if not last:
        shift_ref, scl_ref, wif_ref = refs[:3]
        refs = refs[3:]
    if convert:
        fa_ref, fb_ref, fo_ref = refs[:3]
        refs = refs[3:]
        na_ref, nb_ref, no_ref = refs[-3:]
        na_ref[...] = fa_ref[...].astype(BF)
        nb_ref[...] = fb_ref[...].astype(BF)
        no_ref[...] = fo_ref[...].astype(BF)
    seq = pl.program_id(0) // tiles_per_seq
    br_a = _dot(a_ref[...], wa_ref[0])
    br_b = _dot(bb_ref[...], wb_ref[0])
    merged = (sga_ref[...] * br_a + sgb_ref[...] * br_b).astype(BF)
    y = x_ref[...] + _mod_vec(gate_ref, seq, nseq, lseq) * _dot(merged, wo_ref[0])
    if last:
        r = lax.rsqrt(jnp.mean(y * y, axis=-1, keepdims=True) + EPS)
        refs[0][...] = y * r * nw_ref[0]
    else:
        y_ref, hn_ref, g_ref = refs[:3]
        y_ref[...] = y
        hn = _norm_mod(y, nw_ref[0], _mod_vec(scl_ref, seq, nseq, lseq),
                       _mod_vec(shift_ref, seq, nseq, lseq)).astype(BF)
        hn_ref[...] = hn
        g_ref[...] = _dot(hn, wif_ref[0])


def _out(x2, a2, bb2, sga, sgb, mod_t, wts, nw_all, wif, f32_weights, *, layer, group, tiles_per_seq, t, nseq,
         lseq, last, convert):
    rows = x2.shape[0]
    steps = rows // t
    row_spec = lambda width: pl.BlockSpec((t, width), lambda i: (i, 0))
    in_specs = [row_spec(D_MODEL), row_spec(D_MODEL), row_spec(D_B), row_spec(D_MODEL), row_spec(D_MODEL),
                _mod_spec(layer, 2, group),
                _layer_spec(wts["wa"].shape, 0), _layer_spec(wts["wb"].shape, 0), _layer_spec(wts["wo"].shape, 0),
                pl.BlockSpec((1, 1, D_MODEL), lambda i: (layer + 1, 0, 0))]
    args = [x2, a2, bb2, sga, sgb, mod_t, wts["wa"], wts["wb"], wts["wo"], nw_all]
    if last:
        out_specs = [row_spec(D_MODEL)]
        out_shape = [jax.ShapeDtypeStruct((rows, D_MODEL), F32)]
    else:
        in_specs += [_mod_spec(layer + 1, 0, group), _mod_spec(layer + 1, 1, group),
                     pl.BlockSpec((1, D_MODEL, LANES), lambda i: (layer + 1, 0, 0))]
        args += [mod_t, mod_t, wif]
        out_specs = [row_spec(D_MODEL), row_spec(D_MODEL), row_spec(LANES)]
        out_shape = [jax.ShapeDtypeStruct((rows, D_MODEL), F32),
                     jax.ShapeDtypeStruct((rows, D_MODEL), BF),
                     jax.ShapeDtypeStruct((rows, LANES), F32)]
    if convert:
        for w in f32_weights:
            _, kdim, n = w.shape
            assert kdim % (2 * SUBLANES * steps) == 0
            in_specs.append(pl.BlockSpec((1, kdim // steps, n), lambda i: (layer + 1, i, 0)))
            args.append(w)
            out_specs.append(pl.BlockSpec((1, kdim // steps, n), lambda i: (0, i, 0)))
            out_shape.append(jax.ShapeDtypeStruct((1, kdim, n), BF))
    return pl.pallas_call(
        functools.partial(_out_kernel, last=last, tiles_per_seq=tiles_per_seq, nseq=nseq, lseq=lseq,
                          convert=convert),
        grid=(steps,),
        in_specs=in_specs, out_specs=out_specs, out_shape=out_shape,
        compiler_params=_params(1), name="out",
    )(*args)


def kernel(x_prompt, x_sample, c_prompt, c_sample, state_C, state_n, state_m, state_pool,
           norm_w, w_ada, b_ada, w_in, b_if, head_norm_w, w_pool_mix, pool_scale,
           w_branch_a, w_branch_b, w_out, final_norm_w):
    depth = w_in.shape[0]
    bp, lp, _ = x_prompt.shape
    bs, ls, _ = x_sample.shape
    tp = 256
    t_mlstm = 4 * tp
    assert bp <= SUBLANES and bs == SUBLANES and lp % t_mlstm == 0 and ls >= POOL_HIST and ls % SUBLANES == 0

    w_in_t = jnp.swapaxes(w_in, 1, 2)
    w_rest, wif = _cast_pool_side(w_in_t)
    wts = dict(w_rest=w_rest)
    first_casts = dict(w5a=(w_in_t, 0, N_QKVO), w5b=(w_in_t, N_QKVO, D_MODEL),
                       wa=(w_branch_a, 0, w_branch_a.shape[1]), wb=(w_branch_b, 0, w_branch_b.shape[1]),
                       wo=(w_out, 0, w_out.shape[1]))

    c_all = jnp.zeros((MOD_ROWS, D_MODEL), F32).at[:bp].set(c_prompt).at[SUBLANES:].set(c_sample)
    mod = _modulation(c_all, w_ada, b_ada)
    mod_t = mod.reshape(depth, MOD_ROWS, 3, D_MODEL).transpose(0, 2, 1, 3)

    nw_all = jnp.concatenate([norm_w, final_norm_w[None, :]], axis=0).reshape(depth + 1, 1, D_MODEL)
    hnw = head_norm_w.reshape(depth, 1, D_MODEL)
    ps = pool_scale.reshape(depth, 1, D_B)
    st_n = state_n.reshape(depth, bs, N_HEADS, 1, DH)
    st_m = jnp.broadcast_to(state_m[..., None, None], (depth, bs, N_HEADS, 1, LANES))
    hist0 = jnp.pad(state_pool, ((0, 0), (0, 0), (HIST_ROWS - POOL_HIST, 0), (0, 0)))

    xp = x_prompt.reshape(bp * lp, D_MODEL)
    xs = x_sample.reshape(bs * ls, D_MODEL)
    prompt = dict(group=0, tiles_per_seq=lp // tp, t=tp, nseq=1, lseq=tp)
    sample = dict(group=1, tiles_per_seq=1, t=bs * ls, nseq=bs, lseq=ls)
    hs, gs = _first_h(xs, nw_all, mod_t, wif, **sample)
    hp = gp = None

    outs = {k: [] for k in ("np", "mp", "pp", "ns", "ms", "ps")}
    c1 = c2 = None
    for l in range(depth):
        last = l == depth - 1
        nxt = {}
        res = _pool(xp if l == 0 else hp, wts["w_rest"], w_pool_mix, ps, None,
                    (nw_all, mod_t, wif, 0) if l == 0 else None, w_in_t,
                    layer=l, groups=bp, tiles=lp // tp, lseq=tp, nseq=1, pos0=0, convert=not last,
                    side_casts=tuple(first_casts.values()) if l == 0 else ())
        bb_p, sga_p, sgb_p, hist_p = res[:4]
        if l == 0:
            hp, gp = res[4:6]
            wts.update(zip(first_casts, res[len(res) - len(first_casts):]))
            res = res[:len(res) - len(first_casts)]
        if not last:
            nxt["w_rest"] = res[-1]
        res = _mlstm_prompt(hp, gp, wts, b_if, hnw, c1, w_in_t, depth=depth, layer=l, batch=bp,
                            seq_rows=lp, t=t_mlstm, chunk=tp)
        a_p, c1, n1, m1 = res[:4]
        if not last:
            nxt["w5a"], nxt["w5b"] = res[4:]
        res = _out(xp, a_p, bb_p, sga_p, sgb_p, mod_t, wts, nw_all, wif, (w_branch_a, w_branch_b, w_out),
                   layer=l, last=last, convert=not last, **prompt)
        if last:
            (yp,) = res
        else:
            xp, hp, gp, nxt["wa"], nxt["wb"], nxt["wo"] = res
        outs["np"].append(n1[:, :, 0, :])
        outs["mp"].append(m1[:, :, 0, 0])
        outs["pp"].append(hist_p[:, HIST_ROWS - POOL_HIST:, :])

        a_s, c2, n2, m2 = _mlstm_seqs(hs, gs, wts, b_if, hnw, state_C, st_n, st_m, c2, depth=depth, layer=l,
                                      batch=bs, lseq=ls, per_step=4)
        bb_s, sga_s, sgb_s, hist_s = _pool(hs, wts["w_rest"], w_pool_mix, ps, hist0, None, None, layer=l,
                                           groups=1, tiles=1, lseq=ls, nseq=bs, pos0=PAST_LEN, convert=False)
        res = _out(xs, a_s, bb_s, sga_s, sgb_s, mod_t, wts, nw_all, wif, None, layer=l, last=last,
                   convert=False, **sample)
        if last:
            (ys,) = res
        else:
            xs, hs, gs = res
        outs["ns"].append(n2[:, :, 0, :])
        outs["ms"].append(m2[:, :, 0, 0])
        outs["ps"].append(hist_s[:, HIST_ROWS - POOL_HIST:, :])
        wts = nxt

    st = {k: jnp.stack(v) for k, v in outs.items()}
    return (yp.reshape(bp, lp, D_MODEL), ys.reshape(bs, ls, D_MODEL),
            c1, st["np"], st["mp"], st["pp"],
            c2, st["ns"], st["ms"], st["ps"])
```

```python
import functools

import jax
import jax.numpy as jnp
from jax import lax
from jax.experimental import pallas as pl
from jax.experimental.pallas import tpu as pltpu

D_MODEL = 2048
N_HEADS = 4
DH = D_MODEL // N_HEADS
D_B = D_MODEL // 2
POOL_WINDOWS = (2, 4, 8, 16)
POOL_GROUP = D_B // len(POOL_WINDOWS)
POOL_HIST = 15
HIST_ROWS = 16
PAST_LEN = 2048
EPS = 1e-6
NEG = -1e30
N_QKVO = 4 * D_MODEL
N_QKVOZ = 5 * D_MODEL
N_IF = 2 * N_HEADS
N_REST = 2 * D_B + 2 * D_MODEL

LANES = 128
SUBLANES = 8
VMEM_LIMIT = 60 * 1024 * 1024
MOD_ROWS = 2 * SUBLANES

BF = jnp.bfloat16
F32 = jnp.float32

_dot = functools.partial(jnp.dot, preferred_element_type=F32)


def _dot_nt(a, b):
    return lax.dot_general(a, b, (((1,), (1,)), ((), ())), preferred_element_type=F32)


def _dot_tn(a, b):
    return lax.dot_general(a, b, (((0,), (0,)), ((), ())), preferred_element_type=F32)


def _params(n_axes):
    return pltpu.CompilerParams(dimension_semantics=("arbitrary",) * n_axes,
                                vmem_limit_bytes=VMEM_LIMIT)


def _layer_spec(shape, layer):
    nd = len(shape)
    return pl.BlockSpec((1,) + tuple(shape[1:]), lambda *_: (layer,) + (0,) * (nd - 1),
                        pipeline_mode=pl.Buffered(1))


def _silu(x):
    return x * jax.nn.sigmoid(x)


def _log_sigmoid(x):
    return jnp.minimum(x, 0.0) - jnp.log(1.0 + jnp.exp(-jnp.abs(x)))


def _shifted_rows(a_rows, next_rows):
    return jnp.concatenate([a_rows[N_IF:, :], next_rows], axis=0)


def _shift_specs(layer, nb, block_of_step):
    first = N_QKVOZ // nb
    return [pl.BlockSpec((1, nb, D_MODEL), lambda *g: (layer, first + block_of_step(*g), 0)),
            pl.BlockSpec((1, N_IF, D_MODEL),
                         lambda *g: (layer, (first + block_of_step(*g) + 1) * (nb // N_IF), 0))]


def _cast_shift_kernel(a_ref, b_ref, o_ref):
    o_ref[0] = _shifted_rows(a_ref[0], b_ref[0]).astype(BF)


def _cast_gate_kernel(w_ref, o_ref):
    rows = jnp.concatenate([w_ref[0], jnp.zeros((LANES - N_IF, D_MODEL), F32)], axis=0)
    o_ref[0] = rows.T.astype(BF)


def _cast_pool_side(w_in_t):
    depth = w_in_t.shape[0]
    nb = 1024
    w_rest = pl.pallas_call(
        _cast_shift_kernel, grid=(N_REST // nb,),
        in_specs=_shift_specs(0, nb, lambda j: j),
        out_specs=pl.BlockSpec((1, nb, D_MODEL), lambda j: (0, j, 0)),
        out_shape=jax.ShapeDtypeStruct((1, N_REST, D_MODEL), BF),
        compiler_params=_params(1), name="cast_rest")(w_in_t, w_in_t)
    wif = pl.pallas_call(
        _cast_gate_kernel, grid=(depth,),
        in_specs=[pl.BlockSpec((1, N_IF, D_MODEL), lambda l: (l, N_QKVOZ // N_IF, 0))],
        out_specs=pl.BlockSpec((1, D_MODEL, LANES), lambda l: (l, 0, 0)),
        out_shape=jax.ShapeDtypeStruct((depth, D_MODEL, LANES), BF),
        compiler_params=_params(1), name="cast_gates")(w_in_t)
    return w_rest, wif


def _mod_kernel(c_ref, w_ref, b_ref, o_ref):
    @pl.when(pl.program_id(1) == 0)
    def _init():
        o_ref[0] = jnp.broadcast_to(b_ref[0], o_ref.shape[1:])

    o_ref[0] += _dot(_silu(c_ref[...]).astype(BF), w_ref[0].astype(BF))


def _modulation(c_all, w_ada, b_ada):
    depth, kdim, n = w_ada.shape
    tk = 512
    return pl.pallas_call(
        _mod_kernel,
        grid=(depth, kdim // tk),
        in_specs=[pl.BlockSpec((MOD_ROWS, tk), lambda l, k: (0, k)),
                  pl.BlockSpec((1, tk, n), lambda l, k: (l, k, 0)),
                  pl.BlockSpec((1, 1, n), lambda l, k: (l, 0, 0))],
        out_specs=pl.BlockSpec((1, MOD_ROWS, n), lambda l, k: (l, 0, 0)),
        out_shape=jax.ShapeDtypeStruct((depth, MOD_ROWS, n), F32),
        compiler_params=_params(2), name="modulation",
    )(c_all, w_ada, b_ada.reshape(depth, 1, n))


def _mod_vec(ref, seq, nseq, lseq):
    if nseq == 1:
        return ref[0, 0, pl.ds(seq, 1), :]
    blk = ref[0, 0]
    return jnp.concatenate([jnp.broadcast_to(blk[s:s + 1, :], (lseq, D_MODEL)) for s in range(nseq)], axis=0)


def _mod_spec(layer, kind, group):
    return pl.BlockSpec((1, 1, SUBLANES, D_MODEL), lambda *_: (layer, kind, group, 0))


def _norm_mod(y, nw, scl, shift):
    r = lax.rsqrt(jnp.mean(y * y, axis=-1, keepdims=True) + EPS)
    return (y * r * nw) * (1.0 + scl) + shift


def _gate_columns(g, hd, b_i, b_f):
    lane = lax.broadcasted_iota(jnp.int32, g.shape, 1)
    ig = jnp.sum(jnp.where(lane == hd, g, 0.0), axis=1, keepdims=True) + b_i
    fg = jnp.sum(jnp.where(lane == hd + N_HEADS, g, 0.0), axis=1, keepdims=True) + b_f
    return ig, _log_sigmoid(fg)


def _recurrence(q, k, v, ig_c, lf_c, c_old, n_old, m0):
    t = q.shape[0]
    qb = q.astype(BF)
    vb = v.astype(BF)
    row = lax.broadcasted_iota(jnp.int32, (t, t), 0)
    col = lax.broadcasted_iota(jnp.int32, (t, t), 1)
    causal = col <= row
    eye = col == row
    ig_r = jnp.sum(jnp.where(eye, ig_c, 0.0), axis=0, keepdims=True)
    lf_r = jnp.sum(jnp.where(eye, lf_c, 0.0), axis=0, keepdims=True)
    b_c = jnp.sum(jnp.where(causal, lf_r, 0.0), axis=1, keepdims=True)
    b_r = jnp.sum(jnp.where(row <= col, lf_c, 0.0), axis=0, keepdims=True)
    g_r = ig_r - b_r
    g_c = ig_c - b_c
    big_m = jnp.maximum(jnp.max(jnp.where(causal, g_r, NEG), axis=1, keepdims=True), m0)
    w_intra = jnp.exp(jnp.where(causal, g_r - big_m, NEG))
    a_inter = jnp.exp(m0 - big_m)

    s = _dot_nt(qb, k.astype(BF)) * w_intra
    num = a_inter * _dot(qb, c_old.astype(BF)) + _dot(s.astype(BF), vb)
    den = a_inter * jnp.sum(q * n_old, axis=1, keepdims=True) + jnp.sum(s, axis=1, keepdims=True)
    h = num * (1.0 / jnp.maximum(jnp.abs(den), jnp.exp(-(b_c + big_m))))

    m_last = big_m[t - 1:t, :]
    kw = k * jnp.exp(g_c - m_last)
    decay = jnp.exp(m0 - m_last)
    c_new = decay * c_old + _dot_tn(kw.astype(BF), vb)
    n_new = decay * n_old + jnp.sum(kw, axis=0, keepdims=True)
    return h, c_new, n_new, b_c[t - 1:t, :] + m_last


def _head_out(o, z, h, hnw):
    og = jax.nn.sigmoid(o) * h
    mu = jnp.mean(og, axis=1, keepdims=True)
    dev = og - mu
    var = jnp.mean(dev * dev, axis=1, keepdims=True)
    return (_silu(z) * (dev * lax.rsqrt(var + EPS) * hnw)).astype(BF)


def _mlstm_kernel(*refs, layer, chunk, aliased, convert):
    bif_ref, h_ref, g_ref, wq_ref, wk_ref, wv_ref, wo_ref, wz_ref, hnw_ref = refs[:9]
    refs = refs[9 + aliased:]
    if convert:
        fa_ref, fb_ref, a_ref, c_ref, n_ref, m_ref, na_ref, nb_ref = refs
        na_ref[...] = fa_ref[...].astype(BF)
        nb_ref[...] = fb_ref[...].astype(BF)
    else:
        a_ref, c_ref, n_ref, m_ref = refs
    hd = pl.program_id(0)

    @pl.when(pl.program_id(2) == 0)
    def _init():
        c_ref[0, 0, 0] = jnp.zeros((DH, DH), F32)
        n_ref[0, 0] = jnp.zeros((1, DH), F32)
        m_ref[0, 0] = jnp.zeros((1, LANES), F32)

    hb = h_ref[...]
    q = _dot_nt(hb, wq_ref[0])
    k = _dot_nt(hb, wk_ref[0]) * (DH ** -0.5)
    v = _dot_nt(hb, wv_ref[0])
    ig_c, lf_c = _gate_columns(g_ref[...], hd, bif_ref[layer, hd], bif_ref[layer, N_HEADS + hd])
    c, n, m = c_ref[0, 0, 0], n_ref[0, 0], m_ref[0, 0][:, 0:1]
    hs = []
    for j in range(hb.shape[0] // chunk):
        r = slice(j * chunk, (j + 1) * chunk)
        h, c, n, m = _recurrence(q[r], k[r], v[r], ig_c[r], lf_c[r], c, n, m)
        hs.append(h)
    c_ref[0, 0, 0] = c
    n_ref[0, 0] = n
    m_ref[0, 0] = jnp.broadcast_to(m, (1, LANES))
    a_ref[...] = _head_out(_dot_nt(hb, wo_ref[0]), _dot_nt(hb, wz_ref[0]), jnp.concatenate(hs, axis=0), hnw_ref[0])


def _w5_specs(head_of):
    qkvo = [pl.BlockSpec((1, DH, D_MODEL), lambda *g, grp=grp: (0, grp * N_HEADS + head_of(*g), 0))
            for grp in range(4)]
    return qkvo + [pl.BlockSpec((1, DH, D_MODEL), lambda *g: (0, head_of(*g), 0))]


def _state_out(depth, batch, layer, c_stack, index_map, n_inputs, per_step=1):
    specs = [pl.BlockSpec((1, per_step, 1, DH, DH), lambda *g: (layer,) + index_map(*g)),
             pl.BlockSpec((per_step, 1, 1, DH), index_map),
             pl.BlockSpec((per_step, 1, 1, LANES), index_map)]
    shapes = [jax.ShapeDtypeStruct((depth, batch, N_HEADS, DH, DH), F32),
              jax.ShapeDtypeStruct((batch, N_HEADS, 1, DH), F32),
              jax.ShapeDtypeStruct((batch, N_HEADS, 1, LANES), F32)]
    if c_stack is None:
        return specs, shapes, [], [], {}
    return specs, shapes, [pl.BlockSpec(memory_space=pl.ANY)], [c_stack], {n_inputs: 1}


def _mlstm_prompt(h2, gates, wts, b_if, hnw, c_stack, w_in_t, *, depth, layer, batch, seq_rows, t, chunk):
    nck = seq_rows // t
    rows = batch * seq_rows
    steps = N_HEADS * batch * nck
    convert = layer + 1 < depth
    in_specs = ([pl.BlockSpec(memory_space=pltpu.SMEM),
                 pl.BlockSpec((t, D_MODEL), lambda hd, b, c: (b * nck + c, 0)),
                 pl.BlockSpec((t, LANES), lambda hd, b, c: (b * nck + c, 0))]
                + _w5_specs(lambda hd, b, c: hd)
                + [pl.BlockSpec((1, 1, DH), lambda hd, b, c: (layer, 0, hd))])
    args = [b_if, h2, gates, wts["w5a"], wts["w5a"], wts["w5a"], wts["w5a"], wts["w5b"], hnw]
    st_specs, st_shapes, extra_specs, extra_args, aliases = _state_out(
        depth, batch, layer, c_stack, lambda hd, b, c: (b, hd, 0, 0), len(in_specs))
    in_specs += extra_specs
    args += extra_args
    out_specs = [pl.BlockSpec((t, DH), lambda hd, b, c: (b * nck + c, hd))] + st_specs
    out_shape = [jax.ShapeDtypeStruct((rows, D_MODEL), BF)] + st_shapes
    if convert:
        assert N_QKVO % (2 * SUBLANES * steps) == 0 and D_MODEL % (2 * SUBLANES * steps) == 0
        ra, rb = N_QKVO // steps, D_MODEL // steps
        step = lambda hd, b, c: (hd * batch + b) * nck + c
        in_specs += [pl.BlockSpec((1, ra, D_MODEL), lambda *g: (layer + 1, step(*g), 0)),
                     pl.BlockSpec((1, rb, D_MODEL), lambda *g: (layer + 1, N_QKVO // rb + step(*g), 0))]
        args += [w_in_t, w_in_t]
        out_specs += [pl.BlockSpec((1, ra, D_MODEL), lambda *g: (0, step(*g), 0)),
                      pl.BlockSpec((1, rb, D_MODEL), lambda *g: (0, step(*g), 0))]
        out_shape += [jax.ShapeDtypeStruct((1, N_QKVO, D_MODEL), BF),
                      jax.ShapeDtypeStruct((1, D_MODEL, D_MODEL), BF)]
    return pl.pallas_call(
        functools.partial(_mlstm_kernel, layer=layer, chunk=chunk, aliased=c_stack is not None, convert=convert),
        grid=(N_HEADS, batch, nck),
        in_specs=in_specs, out_specs=out_specs, out_shape=out_shape,
        input_output_aliases=aliases,
        compiler_params=_params(3), name="mlstm",
    )(*args)


def _mlstm_seqs_kernel(bif_ref, h_ref, g_ref, wq_ref, wk_ref, wv_ref, wo_ref, wz_ref, hnw_ref,
                       c0_ref, n0_ref, m0_ref, *rest, layer, lseq, per_step):
    a_ref, c_ref, n_ref, m_ref, q_s, k_s, v_s, h_s, ig_s, lf_s = rest[-10:]
    hd = pl.program_id(0)
    b = pl.program_id(1)

    @pl.when(b == 0)
    def _project():
        hb = h_ref[...]
        q_s[...] = _dot_nt(hb, wq_ref[0])
        k_s[...] = _dot_nt(hb, wk_ref[0]) * (DH ** -0.5)
        v_s[...] = _dot_nt(hb, wv_ref[0])
        ig_c, lf_c = _gate_columns(g_ref[...], hd, bif_ref[layer, hd], bif_ref[layer, N_HEADS + hd])
        ig_s[...] = ig_c
        lf_s[...] = lf_c

    for j in range(per_step):
        rows = pl.ds(pl.multiple_of((b * per_step + j) * lseq, lseq), lseq)
        h, c_new, n_new, m_new = _recurrence(q_s[rows, :], k_s[rows, :], v_s[rows, :], ig_s[rows, :],
                                             lf_s[rows, :], c0_ref[0, j, 0], n0_ref[0, j, 0],
                                             m0_ref[0, j, 0][:, 0:1])
        c_ref[0, j, 0] = c_new
        n_ref[j, 0] = n_new
        m_ref[j, 0] = jnp.broadcast_to(m_new, (1, LANES))
        h_s[rows, :] = h

    @pl.when(b == pl.num_programs(1) - 1)
    def _finish():
        hb = h_ref[...]
        a_ref[...] = _head_out(_dot_nt(hb, wo_ref[0]), _dot_nt(hb, wz_ref[0]), h_s[...], hnw_ref[0])


def _mlstm_seqs(h2, gates, wts, b_if, hnw, state_c, state_n, state_m, c_stack, *, depth, layer, batch, lseq,
                per_step):
    rows = batch * lseq
    in_specs = ([pl.BlockSpec(memory_space=pltpu.SMEM),
                 pl.BlockSpec((rows, D_MODEL), lambda hd, b: (0, 0)),
                 pl.BlockSpec((rows, LANES), lambda hd, b: (0, 0))]
                + _w5_specs(lambda hd, b: hd)
                + [pl.BlockSpec((1, 1, DH), lambda hd, b: (layer, 0, hd)),
                   pl.BlockSpec((1, per_step, 1, DH, DH), lambda hd, b: (layer, b, hd, 0, 0)),
                   pl.BlockSpec((1, per_step, 1, 1, DH), lambda hd, b: (layer, b, hd, 0, 0)),
                   pl.BlockSpec((1, per_step, 1, 1, LANES), lambda hd, b: (layer, b, hd, 0, 0))])
    st_specs, st_shapes, extra_specs, extra_args, aliases = _state_out(
        depth, batch, layer, c_stack, lambda hd, b: (b, hd, 0, 0), len(in_specs), per_step)
    return pl.pallas_call(
        functools.partial(_mlstm_seqs_kernel, layer=layer, lseq=lseq, per_step=per_step),
        grid=(N_HEADS, batch // per_step),
        in_specs=in_specs + extra_specs,
        out_specs=[pl.BlockSpec((rows, DH), lambda hd, b: (0, hd))] + st_specs,
        out_shape=[jax.ShapeDtypeStruct((rows, D_MODEL), BF)] + st_shapes,
        input_output_aliases=aliases,
        scratch_shapes=[pltpu.VMEM((rows, DH), F32)] * 4 + [pltpu.VMEM((rows, 1), F32)] * 2,
        compiler_params=_params(2), name="mlstm_seqs",
    )(b_if, h2, gates, wts["w5a"], wts["w5a"], wts["w5a"], wts["w5a"], wts["w5b"], hnw,
      state_c, state_n, state_m, *extra_args)


def _pool_kernel(*refs, lseq, nseq, pos0, first, has_state, convert, n_casts):
    src_ref, w_ref, wmix_ref, ps_ref = refs[:4]
    refs = refs[4:]
    if first:
        nw_ref, shift_ref, scl_ref, wif_ref = refs[:4]
        refs = refs[4:]
    if has_state:
        hist0_ref, refs = refs[0], refs[1:]
    if convert:
        ra_ref, rb_ref = refs[:2]
        refs = refs[2:]
    cast_src, refs = refs[:n_casts], refs[n_casts:]
    bb_ref, sga_ref, sgb_ref, hist_ref = refs[:4]
    uext = refs[-1]
    refs = refs[4:-1]
    for src, dst in zip(cast_src, refs[len(refs) - n_casts:]):
        dst[...] = src[...].astype(BF)
    refs = refs[:len(refs) - n_casts]
    if convert:
        refs[-1][0] = _shifted_rows(ra_ref[0], rb_ref[0]).astype(BF)
    i = pl.program_id(1)
    seg = HIST_ROWS + lseq

    @pl.when(i == 0)
    def _first():
        for sq in range(nseq):
            uext[sq * seg:sq * seg + HIST_ROWS, :] = (hist0_ref[0, sq] if has_state
                                                       else jnp.zeros((HIST_ROWS, D_B), F32))

    @pl.when(i > 0)
    def _carry():
        uext[0:HIST_ROWS, :] = uext[lseq:seg, :]

    if first:
        seq = pl.program_id(0)
        hb = _norm_mod(src_ref[...], nw_ref[0], _mod_vec(scl_ref, seq, nseq, lseq),
                       _mod_vec(shift_ref, seq, nseq, lseq)).astype(BF)
        refs[0][...] = hb
        refs[1][...] = _dot(hb, wif_ref[0])
    else:
        hb = src_ref[...]
    u = _dot_nt(hb, w_ref[0, 0:D_B, :])
    for sq in range(nseq):
        uext[sq * seg + HIST_ROWS:(sq + 1) * seg, :] = u[sq * lseq:(sq + 1) * lseq, :]
    zb = _dot_nt(hb, w_ref[0, D_B:2 * D_B, :])
    sga_ref[...] = jax.nn.sigmoid(_dot_nt(hb, w_ref[0, 2 * D_B:2 * D_B + D_MODEL, :]))
    sgb_ref[...] = jax.nn.sigmoid(_dot_nt(hb, w_ref[0, 2 * D_B + D_MODEL:, :]))

    in_seq = lax.broadcasted_iota(jnp.int32, (lseq, 1), 0)
    pos = jnp.concatenate([in_seq] * nseq, axis=0) + (i * lseq + pos0 + 1)
    for g, w in enumerate(POOL_WINDOWS):
        sl = slice(g * POOL_GROUP, (g + 1) * POOL_GROUP)
        acc = uext[:, sl]
        shift = 1
        while shift < w:
            acc = acc + pltpu.roll(acc, shift, axis=0)
            shift *= 2
        win = jnp.concatenate([acc[sq * seg + HIST_ROWS:(sq + 1) * seg, :] for sq in range(nseq)], axis=0)
        pooled = win / jnp.minimum(pos, w).astype(F32)
        dlt = (pooled - u[:, sl]).astype(BF)
        mixed = _dot(dlt, wmix_ref[0, g].astype(BF)) * ps_ref[0][:, sl]
        bb_ref[:, sl] = (_silu(zb[:, sl]) * mixed).astype(BF)

    @pl.when(i == pl.num_programs(1) - 1)
    def _hist():
        for sq in range(nseq):
            hist_ref[sq] = uext[sq * seg + lseq:(sq + 1) * seg, :]


def _pool(src, w_rest, wmix, ps, hist0, first_inputs, w_in_t, *, layer, groups, tiles, lseq, nseq, pos0, convert,
          side_casts=()):
    t = lseq * nseq
    rows = groups * tiles * t
    has_state = hist0 is not None
    first = first_inputs is not None
    row_spec = lambda width: pl.BlockSpec((t, width), lambda b, i: (b * tiles + i, 0))
    in_specs = [row_spec(D_MODEL), _layer_spec(w_rest.shape, 0), _layer_spec(wmix.shape, layer),
                _layer_spec(ps.shape, layer)]
    args = [src, w_rest, wmix, ps]
    if first:
        nw_all, mod_t, wif, group = first_inputs
        in_specs += [pl.BlockSpec((1, 1, D_MODEL), lambda b, i: (layer, 0, 0)),
                     _mod_spec(layer, 0, group), _mod_spec(layer, 1, group),
                     pl.BlockSpec((1, D_MODEL, LANES), lambda b, i: (layer, 0, 0))]
        args += [nw_all, mod_t, mod_t, wif]
    if has_state:
        in_specs.append(pl.BlockSpec((1, nseq, HIST_ROWS, D_B), lambda b, i: (layer, b, 0, 0)))
        args.append(hist0)
    out_specs = [row_spec(D_B), row_spec(D_MODEL), row_spec(D_MODEL),
                 pl.BlockSpec((nseq, HIST_ROWS, D_B), lambda b, i: (b, 0, 0))]
    out_shape = [jax.ShapeDtypeStruct((rows, D_B), BF),
                 jax.ShapeDtypeStruct((rows, D_MODEL), F32),
                 jax.ShapeDtypeStruct((rows, D_MODEL), F32),
                 jax.ShapeDtypeStruct((groups * nseq, HIST_ROWS, D_B), F32)]
    if first:
        out_specs += [row_spec(D_MODEL), row_spec(LANES)]
        out_shape += [jax.ShapeDtypeStruct((rows, D_MODEL), BF), jax.ShapeDtypeStruct((rows, LANES), F32)]
    if convert:
        n_blk = N_REST // LANES
        assert groups * tiles >= n_blk
        blk = lambda b, i: jnp.minimum(b * tiles + i, n_blk - 1)
        in_specs += _shift_specs(layer + 1, LANES, blk)
        args += [w_in_t, w_in_t]
        out_specs.append(pl.BlockSpec((1, LANES, D_MODEL), lambda b, i: (0, blk(b, i), 0)))
        out_shape.append(jax.ShapeDtypeStruct((1, N_REST, D_MODEL), BF))
    steps = groups * tiles
    cast_in_specs = []
    for w, first_row, nrows in side_casts:
        assert nrows % (2 * SUBLANES * steps) == 0 and first_row % (nrows // steps) == 0
        rps, ncols = nrows // steps, w.shape[2]
        cast_in_specs.append(pl.BlockSpec((1, rps, ncols),
                                          lambda b, i, base=first_row // rps: (layer, base + b * tiles + i, 0)))
        out_specs.append(pl.BlockSpec((1, rps, ncols), lambda b, i: (0, b * tiles + i, 0)))
        out_shape.append(jax.ShapeDtypeStruct((1, nrows, ncols), BF))
    in_specs += cast_in_specs
    args += [w for w, _, _ in side_casts]
    return pl.pallas_call(
        functools.partial(_pool_kernel, lseq=lseq, nseq=nseq, pos0=pos0, first=first, has_state=has_state,
                          convert=convert, n_casts=len(side_casts)),
        grid=(groups, tiles),
        in_specs=in_specs, out_specs=out_specs, out_shape=out_shape,
        scratch_shapes=[pltpu.VMEM((nseq * (HIST_ROWS + lseq), D_B), F32)],
        compiler_params=_params(2), name="pool",
    )(*args)


def _out_kernel(*refs, last, tiles_per_seq, nseq, lseq, convert):
    x_ref, a_ref, bb_ref, sga_ref, sgb_ref, gate_ref, wa_ref, wb_ref, wo_ref, nw_ref = refs[:10]
    refs = refs[10:]
    if not last:
        shift_ref, scl_ref, wif_ref = refs[:3]
        refs = refs[3:]
    if convert:
        fa_ref, fb_ref, fo_ref = refs[:3]
        refs = refs[3:]
        na_ref, nb_ref, no_ref = refs[-3:]
        na_ref[...] = fa_ref[...].astype(BF)
        nb_ref[...] = fb_ref[...].astype(BF)
        no_ref[...] = fo_ref[...].astype(BF)
    seq = pl.program_id(0) // tiles_per_seq
    br_a = _dot(a_ref[...], wa_ref[0])
    br_b = _dot(bb_ref[...], wb_ref[0])
    merged = (sga_ref[...] * br_a + sgb_ref[...] * br_b).astype(BF)
    y = x_ref[...] + _mod_vec(gate_ref, seq, nseq, lseq) * _dot(merged, wo_ref[0])
    if last:
        r = lax.rsqrt(jnp.mean(y * y, axis=-1, keepdims=True) + EPS)
        refs[0][...] = y * r * nw_ref[0]
    else:
        y_ref, hn_ref, g_ref = refs[:3]
        y_ref[...] = y
        hn = _norm_mod(y, nw_ref[0], _mod_vec(scl_ref, seq, nseq, lseq),
                       _mod_vec(shift_ref, seq, nseq, lseq)).astype(BF)
        hn_ref[...] = hn
        g_ref[...] = _dot(hn, wif_ref[0])


def _out(x2, a2, bb2, sga, sgb, mod_t, wts, nw_all, wif, f32_weights, *, layer, group, tiles_per_seq, t, nseq,
         lseq, last, convert):
    rows = x2.shape[0]
    steps = rows // t
    row_spec = lambda width: pl.BlockSpec((t, width), lambda i: (i, 0))
    in_specs = [row_spec(D_MODEL), row_spec(D_MODEL), row_spec(D_B), row_spec(D_MODEL), row_spec(D_MODEL),
                _mod_spec(layer, 2, group),
                _layer_spec(wts["wa"].shape, 0), _layer_spec(wts["wb"].shape, 0), _layer_spec(wts["wo"].shape, 0),
                pl.BlockSpec((1, 1, D_MODEL), lambda i: (layer + 1, 0, 0))]
    args = [x2, a2, bb2, sga, sgb, mod_t, wts["wa"], wts["wb"], wts["wo"], nw_all]
    if last:
        out_specs = [row_spec(D_MODEL)]
        out_shape = [jax.ShapeDtypeStruct((rows, D_MODEL), F32)]
    else:
        in_specs += [_mod_spec(layer + 1, 0, group), _mod_spec(layer + 1, 1, group),
                     pl.BlockSpec((1, D_MODEL, LANES), lambda i: (layer + 1, 0, 0))]
        args += [mod_t, mod_t, wif]
        out_specs = [row_spec(D_MODEL), row_spec(D_MODEL), row_spec(LANES)]
        out_shape = [jax.ShapeDtypeStruct((rows, D_MODEL), F32),
                     jax.ShapeDtypeStruct((rows, D_MODEL), BF),
                     jax.ShapeDtypeStruct((rows, LANES), F32)]
    if convert:
        for w in f32_weights:
            _, kdim, n = w.shape
            assert kdim % (2 * SUBLANES * steps) == 0
            in_specs.append(pl.BlockSpec((1, kdim // steps, n), lambda i: (layer + 1, i, 0)))
            args.append(w)
            out_specs.append(pl.BlockSpec((1, kdim // steps, n), lambda i: (0, i, 0)))
            out_shape.append(jax.ShapeDtypeStruct((1, kdim, n), BF))
    return pl.pallas_call(
        functools.partial(_out_kernel, last=last, tiles_per_seq=tiles_per_seq, nseq=nseq, lseq=lseq,
                          convert=convert),
        grid=(steps,),
        in_specs=in_specs, out_specs=out_specs, out_shape=out_shape,
        compiler_params=_params(1), name="out",
    )(*args)


def kernel(x_prompt, x_sample, c_prompt, c_sample, state_C, state_n, state_m, state_pool,
           norm_w, w_ada, b_ada, w_in, b_if, head_norm_w, w_pool_mix, pool_scale,
           w_branch_a, w_branch_b, w_out, final_norm_w):
    depth = w_in.shape[0]
    bp, lp, _ = x_prompt.shape
    bs, ls, _ = x_sample.shape
    tp = 256
    t_mlstm = 4 * tp
    assert bp <= SUBLANES and bs == SUBLANES and lp % t_mlstm == 0 and ls >= POOL_HIST and ls % SUBLANES == 0

    w_in_t = jnp.swapaxes(w_in, 1, 2)
    w_rest, wif = _cast_pool_side(w_in_t)
    wts = dict(w_rest=w_rest)
    first_casts = dict(w5a=(w_in_t, 0, N_QKVO), w5b=(w_in_t, N_QKVO, D_MODEL),
                       wa=(w_branch_a, 0, w_branch_a.shape[1]), wb=(w_branch_b, 0, w_branch_b.shape[1]),
                       wo=(w_out, 0, w_out.shape[1]))

    c_all = jnp.zeros((MOD_ROWS, D_MODEL), F32).at[:bp].set(c_prompt).at[SUBLANES:].set(c_sample)
    mod = _modulation(c_all, w_ada, b_ada)
    mod_t = mod.reshape(depth, MOD_ROWS, 3, D_MODEL).transpose(0, 2, 1, 3)

    nw_all = jnp.concatenate([norm_w, final_norm_w[None, :]], axis=0).reshape(depth + 1, 1, D_MODEL)
    hnw = head_norm_w.reshape(depth, 1, D_MODEL)
    ps = pool_scale.reshape(depth, 1, D_B)
    st_n = state_n.reshape(depth, bs, N_HEADS, 1, DH)
    st_m = jnp.broadcast_to(state_m[..., None, None], (depth, bs, N_HEADS, 1, LANES))
    hist0 = jnp.pad(state_pool, ((0, 0), (0, 0), (HIST_ROWS - POOL_HIST, 0), (0, 0)))

    xp = x_prompt.reshape(bp * lp, D_MODEL)
    xs = x_sample.reshape(bs * ls, D_MODEL)
    prompt = dict(group=0, tiles_per_seq=lp // tp, t=tp, nseq=1, lseq=tp)
    sample = dict(group=1, tiles_per_seq=1, t=bs * ls, nseq=bs, lseq=ls)
    hp = gp = hs = gs = None

    outs = {k: [] for k in ("np", "mp", "pp", "ns", "ms", "ps")}
    c1 = c2 = None
    for l in range(depth):
        last = l == depth - 1
        nxt = {}
        res = _pool(xp if l == 0 else hp, wts["w_rest"], w_pool_mix, ps, None,
                    (nw_all, mod_t, wif, 0) if l == 0 else None, w_in_t,
                    layer=l, groups=bp, tiles=lp // tp, lseq=tp, nseq=1, pos0=0, convert=not last,
                    side_casts=tuple(first_casts.values()) if l == 0 else ())
        bb_p, sga_p, sgb_p, hist_p = res[:4]
        if l == 0:
            hp, gp = res[4:6]
            wts.update(zip(first_casts, res[len(res) - len(first_casts):]))
            res = res[:len(res) - len(first_casts)]
        if not last:
            nxt["w_rest"] = res[-1]
        res = _mlstm_prompt(hp, gp, wts, b_if, hnw, c1, w_in_t, depth=depth, layer=l, batch=bp,
                            seq_rows=lp, t=t_mlstm, chunk=tp)
        a_p, c1, n1, m1 = res[:4]
        if not last:
            nxt["w5a"], nxt["w5b"] = res[4:]
        res = _out(xp, a_p, bb_p, sga_p, sgb_p, mod_t, wts, nw_all, wif, (w_branch_a, w_branch_b, w_out),
                   layer=l, last=last, convert=not last, **prompt)
        if last:
            (yp,) = res
        else:
            xp, hp, gp, nxt["wa"], nxt["wb"], nxt["wo"] = res
        outs["np"].append(n1[:, :, 0, :])
        outs["mp"].append(m1[:, :, 0, 0])
        outs["pp"].append(hist_p[:, HIST_ROWS - POOL_HIST:, :])

        res = _pool(xs if l == 0 else hs, wts["w_rest"], w_pool_mix, ps, hist0,
                    (nw_all, mod_t, wif, 1) if l == 0 else None, None, layer=l,
                    groups=1, tiles=1, lseq=ls, nseq=bs, pos0=PAST_LEN, convert=False)
        bb_s, sga_s, sgb_s, hist_s = res[:4]
        if l == 0:
            hs, gs = res[4:]
        a_s, c2, n2, m2 = _mlstm_seqs(hs, gs, wts, b_if, hnw, state_C, st_n, st_m, c2, depth=depth, layer=l,
                                      batch=bs, lseq=ls, per_step=4)
        res = _out(xs, a_s, bb_s, sga_s, sgb_s, mod_t, wts, nw_all, wif, None, layer=l, last=last,
                   convert=False, **sample)
        if last:
            (ys,) = res
        else:
            xs, hs, gs = res
        outs["ns"].append(n2[:, :, 0, :])
        outs["ms"].append(m2[:, :, 0, 0])
        outs["ps"].append(hist_s[:, HIST_ROWS - POOL_HIST:, :])
        wts = nxt

    st = {k: jnp.stack(v) for k, v in outs.items()}
    return (yp.reshape(bp, lp, D_MODEL), ys.reshape(bs, ls, D_MODEL),
            c1, st["np"], st["mp"], st["pp"],
            c2, st["ns"], st["ms"], st["ps"])
```

```python
import functools

import jax
import jax.numpy as jnp
from jax import lax
from jax.experimental import pallas as pl
from jax.experimental.pallas import tpu as pltpu

D_MODEL = 2048
N_HEADS = 4
DH = D_MODEL // N_HEADS
D_B = D_MODEL // 2
POOL_WINDOWS = (2, 4, 8, 16)
POOL_GROUP = D_B // len(POOL_WINDOWS)
POOL_HIST = 15
HIST_ROWS = 16
PAST_LEN = 2048
EPS = 1e-6
NEG = -1e30
N_QKVO = 4 * D_MODEL
N_QKVOZ = 5 * D_MODEL
N_IF = 2 * N_HEADS
N_REST = 2 * D_B + 2 * D_MODEL

LANES = 128
SUBLANES = 8
VMEM_LIMIT = 60 * 1024 * 1024
MOD_ROWS = 2 * SUBLANES

BF = jnp.bfloat16
F32 = jnp.float32

_dot = functools.partial(jnp.dot, preferred_element_type=F32)


def _dot_nt(a, b):
    return lax.dot_general(a, b, (((1,), (1,)), ((), ())), preferred_element_type=F32)


def _dot_tn(a, b):
    return lax.dot_general(a, b, (((0,), (0,)), ((), ())), preferred_element_type=F32)


def _params(n_axes):
    return pltpu.CompilerParams(dimension_semantics=("arbitrary",) * n_axes,
                                vmem_limit_bytes=VMEM_LIMIT)


def _layer_spec(shape, layer):
    nd = len(shape)
    return pl.BlockSpec((1,) + tuple(shape[1:]), lambda *_: (layer,) + (0,) * (nd - 1),
                        pipeline_mode=pl.Buffered(1))


def _silu(x):
    return x * jax.nn.sigmoid(x)


def _log_sigmoid(x):
    return jnp.minimum(x, 0.0) - jnp.log(1.0 + jnp.exp(-jnp.abs(x)))


def _shifted_rows(a_rows, next_rows):
    return jnp.concatenate([a_rows[N_IF:, :], next_rows], axis=0)


def _shift_specs(layer, nb, block_of_step):
    first = N_QKVOZ // nb
    return [pl.BlockSpec((1, nb, D_MODEL), lambda *g: (layer, first + block_of_step(*g), 0)),
            pl.BlockSpec((1, N_IF, D_MODEL),
                         lambda *g: (layer, (first + block_of_step(*g) + 1) * (nb // N_IF), 0))]


def _cast_shift_kernel(a_ref, b_ref, o_ref):
    o_ref[0] = _shifted_rows(a_ref[0], b_ref[0]).astype(BF)


def _cast_gate_kernel(w_ref, o_ref):
    rows = jnp.concatenate([w_ref[0], jnp.zeros((LANES - N_IF, D_MODEL), F32)], axis=0)
    o_ref[0] = rows.T.astype(BF)


def _cast_pool_side(w_in_t):
    depth = w_in_t.shape[0]
    nb = 1024
    w_rest = pl.pallas_call(
        _cast_shift_kernel, grid=(N_REST // nb,),
        in_specs=_shift_specs(0, nb, lambda j: j),
        out_specs=pl.BlockSpec((1, nb, D_MODEL), lambda j: (0, j, 0)),
        out_shape=jax.ShapeDtypeStruct((1, N_REST, D_MODEL), BF),
        compiler_params=_params(1), name="cast_rest")(w_in_t, w_in_t)
    wif = pl.pallas_call(
        _cast_gate_kernel, grid=(depth,),
        in_specs=[pl.BlockSpec((1, N_IF, D_MODEL), lambda l: (l, N_QKVOZ // N_IF, 0))],
        out_specs=pl.BlockSpec((1, D_MODEL, LANES), lambda l: (l, 0, 0)),
        out_shape=jax.ShapeDtypeStruct((depth, D_MODEL, LANES), BF),
        compiler_params=_params(1), name="cast_gates")(w_in_t)
    return w_rest, wif


def _mod_kernel(c_ref, w_ref, b_ref, o_ref):
    @pl.when(pl.program_id(1) == 0)
    def _init():
        o_ref[0] = jnp.broadcast_to(b_ref[0], o_ref.shape[1:])

    o_ref[0] += _dot(_silu(c_ref[...]).astype(BF), w_ref[0].astype(BF))


def _modulation(c_all, w_ada, b_ada):
    depth, kdim, n = w_ada.shape
    tk = 512
    return pl.pallas_call(
        _mod_kernel,
        grid=(depth, kdim // tk),
        in_specs=[pl.BlockSpec((MOD_ROWS, tk), lambda l, k: (0, k)),
                  pl.BlockSpec((1, tk, n), lambda l, k: (l, k, 0)),
                  pl.BlockSpec((1, 1, n), lambda l, k: (l, 0, 0))],
        out_specs=pl.BlockSpec((1, MOD_ROWS, n), lambda l, k: (l, 0, 0)),
        out_shape=jax.ShapeDtypeStruct((depth, MOD_ROWS, n), F32),
        compiler_params=_params(2), name="modulation",
    )(c_all, w_ada, b_ada.reshape(depth, 1, n))


def _mod_vec(ref, seq, nseq, lseq):
    if nseq == 1:
        return ref[0, 0, pl.ds(seq, 1), :]
    blk = ref[0, 0]
    return jnp.concatenate([jnp.broadcast_to(blk[s:s + 1, :], (lseq, D_MODEL)) for s in range(nseq)], axis=0)


def _mod_spec(layer, kind, group):
    return pl.BlockSpec((1, 1, SUBLANES, D_MODEL), lambda *_: (layer, kind, group, 0))


def _norm_mod(y, nw, scl, shift):
    r = lax.rsqrt(jnp.mean(y * y, axis=-1, keepdims=True) + EPS)
    return (y * r * nw) * (1.0 + scl) + shift


def _gate_columns(g, hd, b_i, b_f):
    lane = lax.broadcasted_iota(jnp.int32, g.shape, 1)
    ig = jnp.sum(jnp.where(lane == hd, g, 0.0), axis=1, keepdims=True) + b_i
    fg = jnp.sum(jnp.where(lane == hd + N_HEADS, g, 0.0), axis=1, keepdims=True) + b_f
    return ig, _log_sigmoid(fg)


def _recurrence(q, k, v, ig_c, lf_c, c_old, n_old, m0):
    t = q.shape[0]
    qb = q.astype(BF)
    vb = v.astype(BF)
    row = lax.broadcasted_iota(jnp.int32, (t, t), 0)
    col = lax.broadcasted_iota(jnp.int32, (t, t), 1)
    causal = col <= row
    eye = col == row
    ig_r = jnp.sum(jnp.where(eye, ig_c, 0.0), axis=0, keepdims=True)
    lf_r = jnp.sum(jnp.where(eye, lf_c, 0.0), axis=0, keepdims=True)
    b_c = jnp.sum(jnp.where(causal, lf_r, 0.0), axis=1, keepdims=True)
    b_r = jnp.sum(jnp.where(row <= col, lf_c, 0.0), axis=0, keepdims=True)
    g_r = ig_r - b_r
    g_c = ig_c - b_c
    big_m = jnp.maximum(jnp.max(jnp.where(causal, g_r, NEG), axis=1, keepdims=True), m0)
    w_intra = jnp.exp(jnp.where(causal, g_r - big_m, NEG))
    a_inter = jnp.exp(m0 - big_m)

    s = _dot_nt(qb, k.astype(BF)) * w_intra
    num = a_inter * _dot(qb, c_old.astype(BF)) + _dot(s.astype(BF), vb)
    den = a_inter * jnp.sum(q * n_old, axis=1, keepdims=True) + jnp.sum(s, axis=1, keepdims=True)
    h = num * (1.0 / jnp.maximum(jnp.abs(den), jnp.exp(-(b_c + big_m))))

    m_last = big_m[t - 1:t, :]
    kw = k * jnp.exp(g_c - m_last)
    decay = jnp.exp(m0 - m_last)
    c_new = decay * c_old + _dot_tn(kw.astype(BF), vb)
    n_new = decay * n_old + jnp.sum(kw, axis=0, keepdims=True)
    return h, c_new, n_new, b_c[t - 1:t, :] + m_last


def _head_out(o, z, h, hnw):
    og = jax.nn.sigmoid(o) * h
    mu = jnp.mean(og, axis=1, keepdims=True)
    dev = og - mu
    var = jnp.mean(dev * dev, axis=1, keepdims=True)
    return (_silu(z) * (dev * lax.rsqrt(var + EPS) * hnw)).astype(BF)


def _mlstm_kernel(*refs, layer, chunk, aliased, convert):
    bif_ref, h_ref, g_ref, wq_ref, wk_ref, wv_ref, wo_ref, wz_ref, hnw_ref = refs[:9]
    refs = refs[9 + aliased:]
    if convert:
        fa_ref, fb_ref, a_ref, c_ref, n_ref, m_ref, na_ref, nb_ref = refs
        na_ref[...] = fa_ref[...].astype(BF)
        nb_ref[...] = fb_ref[...].astype(BF)
    else:
        a_ref, c_ref, n_ref, m_ref = refs
    hd = pl.program_id(0)

    @pl.when(pl.program_id(2) == 0)
    def _init():
        c_ref[0, 0, 0] = jnp.zeros((DH, DH), F32)
        n_ref[0, 0] = jnp.zeros((1, DH), F32)
        m_ref[0, 0] = jnp.zeros((1, LANES), F32)

    hb = h_ref[...]
    q = _dot_nt(hb, wq_ref[0])
    k = _dot_nt(hb, wk_ref[0]) * (DH ** -0.5)
    v = _dot_nt(hb, wv_ref[0])
    ig_c, lf_c = _gate_columns(g_ref[...], hd, bif_ref[layer, hd], bif_ref[layer, N_HEADS + hd])
    c, n, m = c_ref[0, 0, 0], n_ref[0, 0], m_ref[0, 0][:, 0:1]
    hs = []
    for j in range(hb.shape[0] // chunk):
        r = slice(j * chunk, (j + 1) * chunk)
        h, c, n, m = _recurrence(q[r], k[r], v[r], ig_c[r], lf_c[r], c, n, m)
        hs.append(h)
    c_ref[0, 0, 0] = c
    n_ref[0, 0] = n
    m_ref[0, 0] = jnp.broadcast_to(m, (1, LANES))
    a_ref[...] = _head_out(_dot_nt(hb, wo_ref[0]), _dot_nt(hb, wz_ref[0]), jnp.concatenate(hs, axis=0), hnw_ref[0])


def _w5_specs(head_of):
    qkvo = [pl.BlockSpec((1, DH, D_MODEL), lambda *g, grp=grp: (0, grp * N_HEADS + head_of(*g), 0))
            for grp in range(4)]
    return qkvo + [pl.BlockSpec((1, DH, D_MODEL), lambda *g: (0, head_of(*g), 0))]


def _state_out(depth, batch, layer, c_stack, index_map, n_inputs, per_step=1):
    specs = [pl.BlockSpec((1, per_step, 1, DH, DH), lambda *g: (layer,) + index_map(*g)),
             pl.BlockSpec((per_step, 1, 1, DH), index_map),
             pl.BlockSpec((per_step, 1, 1, LANES), index_map)]
    shapes = [jax.ShapeDtypeStruct((depth, batch, N_HEADS, DH, DH), F32),
              jax.ShapeDtypeStruct((batch, N_HEADS, 1, DH), F32),
              jax.ShapeDtypeStruct((batch, N_HEADS, 1, LANES), F32)]
    if c_stack is None:
        return specs, shapes, [], [], {}
    return specs, shapes, [pl.BlockSpec(memory_space=pl.ANY)], [c_stack], {n_inputs: 1}


def _mlstm_prompt(h2, gates, wts, b_if, hnw, c_stack, w_in_t, *, depth, layer, batch, seq_rows, t, chunk):
    nck = seq_rows // t
    rows = batch * seq_rows
    steps = N_HEADS * batch * nck
    convert = layer + 1 < depth
    in_specs = ([pl.BlockSpec(memory_space=pltpu.SMEM),
                 pl.BlockSpec((t, D_MODEL), lambda hd, b, c: (b * nck + c, 0)),
                 pl.BlockSpec((t, LANES), lambda hd, b, c: (b * nck + c, 0))]
                + _w5_specs(lambda hd, b, c: hd)
                + [pl.BlockSpec((1, 1, DH), lambda hd, b, c: (layer, 0, hd))])
    args = [b_if, h2, gates, wts["w5a"], wts["w5a"], wts["w5a"], wts["w5a"], wts["w5b"], hnw]
    st_specs, st_shapes, extra_specs, extra_args, aliases = _state_out(
        depth, batch, layer, c_stack, lambda hd, b, c: (b, hd, 0, 0), len(in_specs))
    in_specs += extra_specs
    args += extra_args
    out_specs = [pl.BlockSpec((t, DH), lambda hd, b, c: (b * nck + c, hd))] + st_specs
    out_shape = [jax.ShapeDtypeStruct((rows, D_MODEL), BF)] + st_shapes
    if convert:
        assert N_QKVO % (2 * SUBLANES * steps) == 0 and D_MODEL % (2 * SUBLANES * steps) == 0
        ra, rb = N_QKVO // steps, D_MODEL // steps
        step = lambda hd, b, c: (hd * batch + b) * nck + c
        in_specs += [pl.BlockSpec((1, ra, D_MODEL), lambda *g: (layer + 1, step(*g), 0)),
                     pl.BlockSpec((1, rb, D_MODEL), lambda *g: (layer + 1, N_QKVO // rb + step(*g), 0))]
        args += [w_in_t, w_in_t]
        out_specs += [pl.BlockSpec((1, ra, D_MODEL), lambda *g: (0, step(*g), 0)),
                      pl.BlockSpec((1, rb, D_MODEL), lambda *g: (0, step(*g), 0))]
        out_shape += [jax.ShapeDtypeStruct((1, N_QKVO, D_MODEL), BF),
                      jax.ShapeDtypeStruct((1, D_MODEL, D_MODEL), BF)]
    return pl.pallas_call(
        functools.partial(_mlstm_kernel, layer=layer, chunk=chunk, aliased=c_stack is not None, convert=convert),
        grid=(N_HEADS, batch, nck),
        in_specs=in_specs, out_specs=out_specs, out_shape=out_shape,
        input_output_aliases=aliases,
        compiler_params=_params(3), name="mlstm",
    )(*args)


def _mlstm_seqs_kernel(bif_ref, h_ref, g_ref, wq_ref, wk_ref, wv_ref, wo_ref, wz_ref, hnw_ref,
                       c0_ref, n0_ref, m0_ref, *rest, layer, lseq, per_step):
    a_ref, c_ref, n_ref, m_ref, q_s, k_s, v_s, h_s, ig_s, lf_s = rest[-10:]
    hd = pl.program_id(0)
    b = pl.program_id(1)

    @pl.when(b == 0)
    def _project():
        hb = h_ref[...]
        q_s[...] = _dot_nt(hb, wq_ref[0])
        k_s[...] = _dot_nt(hb, wk_ref[0]) * (DH ** -0.5)
        v_s[...] = _dot_nt(hb, wv_ref[0])
        ig_c, lf_c = _gate_columns(g_ref[...], hd, bif_ref[layer, hd], bif_ref[layer, N_HEADS + hd])
        ig_s[...] = ig_c
        lf_s[...] = lf_c

    for j in range(per_step):
        rows = pl.ds(pl.multiple_of((b * per_step + j) * lseq, lseq), lseq)
        h, c_new, n_new, m_new = _recurrence(q_s[rows, :], k_s[rows, :], v_s[rows, :], ig_s[rows, :],
                                             lf_s[rows, :], c0_ref[0, j, 0], n0_ref[0, j, 0],
                                             m0_ref[0, j, 0][:, 0:1])
        c_ref[0, j, 0] = c_new
        n_ref[j, 0] = n_new
        m_ref[j, 0] = jnp.broadcast_to(m_new, (1, LANES))
        h_s[rows, :] = h

    @pl.when(b == pl.num_programs(1) - 1)
    def _finish():
        hb = h_ref[...]
        a_ref[...] = _head_out(_dot_nt(hb, wo_ref[0]), _dot_nt(hb, wz_ref[0]), h_s[...], hnw_ref[0])


def _mlstm_seqs(h2, gates, wts, b_if, hnw, state_c, state_n, state_m, c_stack, *, depth, layer, batch, lseq,
                per_step):
    rows = batch * lseq
    in_specs = ([pl.BlockSpec(memory_space=pltpu.SMEM),
                 pl.BlockSpec((rows, D_MODEL), lambda hd, b: (0, 0)),
                 pl.BlockSpec((rows, LANES), lambda hd, b: (0, 0))]
                + _w5_specs(lambda hd, b: hd)
                + [pl.BlockSpec((1, 1, DH), lambda hd, b: (layer, 0, hd)),
                   pl.BlockSpec((1, per_step, 1, DH, DH), lambda hd, b: (layer, b, hd, 0, 0)),
                   pl.BlockSpec((1, per_step, 1, 1, DH), lambda hd, b: (layer, b, hd, 0, 0)),
                   pl.BlockSpec((1, per_step, 1, 1, LANES), lambda hd, b: (layer, b, hd, 0, 0))])
    st_specs, st_shapes, extra_specs, extra_args, aliases = _state_out(
        depth, batch, layer, c_stack, lambda hd, b: (b, hd, 0, 0), len(in_specs), per_step)
    return pl.pallas_call(
        functools.partial(_mlstm_seqs_kernel, layer=layer, lseq=lseq, per_step=per_step),
        grid=(N_HEADS, batch // per_step),
        in_specs=in_specs + extra_specs,
        out_specs=[pl.BlockSpec((rows, DH), lambda hd, b: (0, hd))] + st_specs,
        out_shape=[jax.ShapeDtypeStruct((rows, D_MODEL), BF)] + st_shapes,
        input_output_aliases=aliases,
        scratch_shapes=[pltpu.VMEM((rows, DH), F32)] * 4 + [pltpu.VMEM((rows, 1), F32)] * 2,
        compiler_params=_params(2), name="mlstm_seqs",
    )(b_if, h2, gates, wts["w5a"], wts["w5a"], wts["w5a"], wts["w5a"], wts["w5b"], hnw,
      state_c, state_n, state_m, *extra_args)


def _pool_kernel(*refs, lseq, nseq, pos0, first, has_state, convert, n_casts):
    src_ref, w_ref, wmix_ref, ps_ref = refs[:4]
    refs = refs[4:]
    if first:
        nw_ref, shift_ref, scl_ref, wif_ref = refs[:4]
        refs = refs[4:]
    if has_state:
        hist0_ref, refs = refs[0], refs[1:]
    if convert:
        ra_ref, rb_ref = refs[:2]
        refs = refs[2:]
    cast_src, refs = refs[:n_casts], refs[n_casts:]
    bb_ref, sga_ref, sgb_ref, hist_ref = refs[:4]
    uext = refs[-1]
    refs = refs[4:-1]
    for src, dst in zip(cast_src, refs[len(refs) - n_casts:]):
        dst[...] = src[...].astype(BF)
    refs = refs[:len(refs) - n_casts]
    if convert:
        refs[-1][0] = _shifted_rows(ra_ref[0], rb_ref[0]).astype(BF)
    if first:
        hb = _first_input(src_ref, nw_ref, shift_ref, scl_ref, wif_ref, refs[0], refs[1], pl.program_id(0), nseq, lseq)
    else:
        hb = src_ref[...]
    u = _dot_nt(hb, w_ref[0, 0:D_B, :])
    zb = _dot_nt(hb, w_ref[0, D_B:2 * D_B, :])
    sga_ref[...] = jax.nn.sigmoid(_dot_nt(hb, w_ref[0, 2 * D_B:2 * D_B + D_MODEL, :]))
    sgb_ref[...] = jax.nn.sigmoid(_dot_nt(hb, w_ref[0, 2 * D_B + D_MODEL:, :]))
    _pool_mix(u, zb, uext, hist0_ref if has_state else None, wmix_ref, ps_ref, bb_ref, hist_ref,
              tile=pl.program_id(1), last_tile=pl.num_programs(1) - 1, lseq=lseq, nseq=nseq, pos0=pos0)


def _first_input(x_ref, nw_ref, shift_ref, scl_ref, wif_ref, h_ref, g_ref, seq, nseq, lseq):
    hb = _norm_mod(x_ref[...], nw_ref[0], _mod_vec(scl_ref, seq, nseq, lseq),
                   _mod_vec(shift_ref, seq, nseq, lseq)).astype(BF)
    h_ref[...] = hb
    g_ref[...] = _dot(hb, wif_ref[0])
    return hb


def _pool_mix(u, zb, uext, hist0_ref, wmix_ref, ps_ref, bb_ref, hist_ref, *, tile, last_tile, lseq, nseq, pos0):
    seg = HIST_ROWS + lseq

    def start():
        for sq in range(nseq):
            uext[sq * seg:sq * seg + HIST_ROWS, :] = (jnp.zeros((HIST_ROWS, D_B), F32) if hist0_ref is None
                                                       else hist0_ref[0, sq])

    def new_history():
        for sq in range(nseq):
            hist_ref[sq] = uext[sq * seg + lseq:(sq + 1) * seg, :]

    if tile is None:
        start()
    else:
        pl.when(tile == 0)(start)

        @pl.when(tile > 0)
        def _carry():
            uext[0:HIST_ROWS, :] = uext[lseq:seg, :]

    for sq in range(nseq):
        uext[sq * seg + HIST_ROWS:(sq + 1) * seg, :] = u[sq * lseq:(sq + 1) * lseq, :]
    in_seq = lax.broadcasted_iota(jnp.int32, (lseq, 1), 0)
    pos = jnp.concatenate([in_seq] * nseq, axis=0) + ((0 if tile is None else tile * lseq) + pos0 + 1)
    for g, w in enumerate(POOL_WINDOWS):
        sl = slice(g * POOL_GROUP, (g + 1) * POOL_GROUP)
        acc = uext[:, sl]
        shift = 1
        while shift < w:
            acc = acc + pltpu.roll(acc, shift, axis=0)
            shift *= 2
        win = jnp.concatenate([acc[sq * seg + HIST_ROWS:(sq + 1) * seg, :] for sq in range(nseq)], axis=0)
        pooled = win / jnp.minimum(pos, w).astype(F32)
        dlt = (pooled - u[:, sl]).astype(BF)
        mixed = _dot(dlt, wmix_ref[0, g].astype(BF)) * ps_ref[0][:, sl]
        bb_ref[:, sl] = (_silu(zb[:, sl]) * mixed).astype(BF)

    if tile is None:
        new_history()
    else:
        pl.when(tile == last_tile)(new_history)


def _pool_stream_kernel(*refs, lseq, nseq, pos0, first, nb):
    src_ref, w_ref, wmix_ref, ps_ref = refs[:4]
    refs = refs[4:]
    if first:
        nw_ref, shift_ref, scl_ref, wif_ref = refs[:4]
        refs = refs[4:]
    hist0_ref, bb_ref, sga_ref, sgb_ref, hist_ref = refs[:5]
    uext, hb_s, proj_s = refs[-3:]
    refs = refs[5:-3]
    j = pl.program_id(0)

    @pl.when(j == 0)
    def _input():
        if first:
            _first_input(src_ref, nw_ref, shift_ref, scl_ref, wif_ref, hb_s, refs[1], 0, nseq, lseq)
            refs[0][...] = hb_s[...]
        else:
            hb_s[...] = src_ref[...]

    proj_s[:, pl.ds(pl.multiple_of(j * nb, nb), nb)] = _dot_nt(hb_s[...], w_ref[0])

    @pl.when(j == pl.num_programs(0) - 1)
    def _mix():
        sga_ref[...] = jax.nn.sigmoid(proj_s[:, 2 * D_B:2 * D_B + D_MODEL])
        sgb_ref[...] = jax.nn.sigmoid(proj_s[:, 2 * D_B + D_MODEL:])
        _pool_mix(proj_s[:, 0:D_B], proj_s[:, D_B:2 * D_B], uext, hist0_ref, wmix_ref, ps_ref, bb_ref, hist_ref,
                  tile=None, last_tile=None, lseq=lseq, nseq=nseq, pos0=pos0)


def _pool_stream(src, w_rest, wmix, ps, hist0, first_inputs, *, layer, lseq, nseq, pos0):
    t = lseq * nseq
    nb = D_B
    first = first_inputs is not None
    whole = lambda shape: pl.BlockSpec(shape, lambda j: (0,) * len(shape))
    in_specs = [whole((t, D_MODEL)), pl.BlockSpec((1, nb, D_MODEL), lambda j: (0, j, 0)),
                _layer_spec(wmix.shape, layer), _layer_spec(ps.shape, layer)]
    args = [src, w_rest, wmix, ps]
    if first:
        nw_all, mod_t, wif, group = first_inputs
        in_specs += [pl.BlockSpec((1, 1, D_MODEL), lambda j: (layer, 0, 0)),
                     _mod_spec(layer, 0, group), _mod_spec(layer, 1, group),
                     pl.BlockSpec((1, D_MODEL, LANES), lambda j: (layer, 0, 0))]
        args += [nw_all, mod_t, mod_t, wif]
    in_specs.append(pl.BlockSpec((1, nseq, HIST_ROWS, D_B), lambda j: (layer, 0, 0, 0)))
    args.append(hist0)
    out_specs = [whole((t, D_B)), whole((t, D_MODEL)), whole((t, D_MODEL)), whole((nseq, HIST_ROWS, D_B))]
    out_shape = [jax.ShapeDtypeStruct((t, D_B), BF), jax.ShapeDtypeStruct((t, D_MODEL), F32),
                 jax.ShapeDtypeStruct((t, D_MODEL), F32), jax.ShapeDtypeStruct((nseq, HIST_ROWS, D_B), F32)]
    if first:
        out_specs += [whole((t, D_MODEL)), whole((t, LANES))]
        out_shape += [jax.ShapeDtypeStruct((t, D_MODEL), BF), jax.ShapeDtypeStruct((t, LANES), F32)]
    return pl.pallas_call(
        functools.partial(_pool_stream_kernel, lseq=lseq, nseq=nseq, pos0=pos0, first=first, nb=nb),
        grid=(N_REST // nb,),
        in_specs=in_specs, out_specs=out_specs, out_shape=out_shape,
        scratch_shapes=[pltpu.VMEM((nseq * (HIST_ROWS + lseq), D_B), F32), pltpu.VMEM((t, D_MODEL), BF),
                        pltpu.VMEM((t, N_REST), F32)],
        compiler_params=_params(1), name="pool_stream",
    )(*args)


def _pool(src, w_rest, wmix, ps, hist0, first_inputs, w_in_t, *, layer, groups, tiles, lseq, nseq, pos0, convert,
          side_casts=()):
    t = lseq * nseq
    rows = groups * tiles * t
    has_state = hist0 is not None
    first = first_inputs is not None
    row_spec = lambda width: pl.BlockSpec((t, width), lambda b, i: (b * tiles + i, 0))
    in_specs = [row_spec(D_MODEL), _layer_spec(w_rest.shape, 0), _layer_spec(wmix.shape, layer),
                _layer_spec(ps.shape, layer)]
    args = [src, w_rest, wmix, ps]
    if first:
        nw_all, mod_t, wif, group = first_inputs
        in_specs += [pl.BlockSpec((1, 1, D_MODEL), lambda b, i: (layer, 0, 0)),
                     _mod_spec(layer, 0, group), _mod_spec(layer, 1, group),
                     pl.BlockSpec((1, D_MODEL, LANES), lambda b, i: (layer, 0, 0))]
        args += [nw_all, mod_t, mod_t, wif]
    if has_state:
        in_specs.append(pl.BlockSpec((1, nseq, HIST_ROWS, D_B), lambda b, i: (layer, b, 0, 0)))
        args.append(hist0)
    out_specs = [row_spec(D_B), row_spec(D_MODEL), row_spec(D_MODEL),
                 pl.BlockSpec((nseq, HIST_ROWS, D_B), lambda b, i: (b, 0, 0))]
    out_shape = [jax.ShapeDtypeStruct((rows, D_B), BF),
                 jax.ShapeDtypeStruct((rows, D_MODEL), F32),
                 jax.ShapeDtypeStruct((rows, D_MODEL), F32),
                 jax.ShapeDtypeStruct((groups * nseq, HIST_ROWS, D_B), F32)]
    if first:
        out_specs += [row_spec(D_MODEL), row_spec(LANES)]
        out_shape += [jax.ShapeDtypeStruct((rows, D_MODEL), BF), jax.ShapeDtypeStruct((rows, LANES), F32)]
    if convert:
        n_blk = N_REST // LANES
        assert groups * tiles >= n_blk
        blk = lambda b, i: jnp.minimum(b * tiles + i, n_blk - 1)
        in_specs += _shift_specs(layer + 1, LANES, blk)
        args += [w_in_t, w_in_t]
        out_specs.append(pl.BlockSpec((1, LANES, D_MODEL), lambda b, i: (0, blk(b, i), 0)))
        out_shape.append(jax.ShapeDtypeStruct((1, N_REST, D_MODEL), BF))
    steps = groups * tiles
    cast_in_specs = []
    for w, first_row, nrows in side_casts:
        assert nrows % (2 * SUBLANES * steps) == 0 and first_row % (nrows // steps) == 0
        rps, ncols = nrows // steps, w.shape[2]
        cast_in_specs.append(pl.BlockSpec((1, rps, ncols),
                                          lambda b, i, base=first_row // rps: (layer, base + b * tiles + i, 0)))
        out_specs.append(pl.BlockSpec((1, rps, ncols), lambda b, i: (0, b * tiles + i, 0)))
        out_shape.append(jax.ShapeDtypeStruct((1, nrows, ncols), BF))
    in_specs += cast_in_specs
    args += [w for w, _, _ in side_casts]
    return pl.pallas_call(
        functools.partial(_pool_kernel, lseq=lseq, nseq=nseq, pos0=pos0, first=first, has_state=has_state,
                          convert=convert, n_casts=len(side_casts)),
        grid=(groups, tiles),
        in_specs=in_specs, out_specs=out_specs, out_shape=out_shape,
        scratch_shapes=[pltpu.VMEM((nseq * (HIST_ROWS + lseq), D_B), F32)],
        compiler_params=_params(2), name="pool",
    )(*args)


def _out_kernel(*refs, last, tiles_per_seq, nseq, lseq, convert):
    x_ref, a_ref, bb_ref, sga_ref, sgb_ref, gate_ref, wa_ref, wb_ref, wo_ref, nw_ref = refs[:10]
    refs = refs[10:]
    if not last:
        shift_ref, scl_ref, wif_ref = refs[:3]
        refs = refs[3:]
    if convert:
        fa_ref, fb_ref, fo_ref = refs[:3]
        refs = refs[3:]
        na_ref, nb_ref, no_ref = refs[-3:]
        na_ref[...] = fa_ref[...].astype(BF)
        nb_ref[...] = fb_ref[...].astype(BF)
        no_ref[...] = fo_ref[...].astype(BF)
    seq = pl.program_id(0) // tiles_per_seq
    br_a = _dot(a_ref[...], wa_ref[0])
    br_b = _dot(bb_ref[...], wb_ref[0])
    merged = (sga_ref[...] * br_a + sgb_ref[...] * br_b).astype(BF)
    y = x_ref[...] + _mod_vec(gate_ref, seq, nseq, lseq) * _dot(merged, wo_ref[0])
    if last:
        r = lax.rsqrt(jnp.mean(y * y, axis=-1, keepdims=True) + EPS)
        refs[0][...] = y * r * nw_ref[0]
    else:
        y_ref, hn_ref, g_ref = refs[:3]
        y_ref[...] = y
        hn = _norm_mod(y, nw_ref[0], _mod_vec(scl_ref, seq, nseq, lseq),
                       _mod_vec(shift_ref, seq, nseq, lseq)).astype(BF)
        hn_ref[...] = hn
        g_ref[...] = _dot(hn, wif_ref[0])


def _out(x2, a2, bb2, sga, sgb, mod_t, wts, nw_all, wif, f32_weights, *, layer, group, tiles_per_seq, t, nseq,
         lseq, last, convert):
    rows = x2.shape[0]
    steps = rows // t
    row_spec = lambda width: pl.BlockSpec((t, width), lambda i: (i, 0))
    in_specs = [row_spec(D_MODEL), row_spec(D_MODEL), row_spec(D_B), row_spec(D_MODEL), row_spec(D_MODEL),
                _mod_spec(layer, 2, group),
                _layer_spec(wts["wa"].shape, 0), _layer_spec(wts["wb"].shape, 0), _layer_spec(wts["wo"].shape, 0),
                pl.BlockSpec((1, 1, D_MODEL), lambda i: (layer + 1, 0, 0))]
    args = [x2, a2, bb2, sga, sgb, mod_t, wts["wa"], wts["wb"], wts["wo"], nw_all]
    if last:
        out_specs = [row_spec(D_MODEL)]
        out_shape = [jax.ShapeDtypeStruct((rows, D_MODEL), F32)]
    else:
        in_specs += [_mod_spec(layer + 1, 0, group), _mod_spec(layer + 1, 1, group),
                     pl.BlockSpec((1, D_MODEL, LANES), lambda i: (layer + 1, 0, 0))]
        args += [mod_t, mod_t, wif]
        out_specs = [row_spec(D_MODEL), row_spec(D_MODEL), row_spec(LANES)]
        out_shape = [jax.ShapeDtypeStruct((rows, D_MODEL), F32),
                     jax.ShapeDtypeStruct((rows, D_MODEL), BF),
                     jax.ShapeDtypeStruct((rows, LANES), F32)]
    if convert:
        for w in f32_weights:
            _, kdim, n = w.shape
            assert kdim % (2 * SUBLANES * steps) == 0
            in_specs.append(pl.BlockSpec((1, kdim // steps, n), lambda i: (layer + 1, i, 0)))
            args.append(w)
            out_specs.append(pl.BlockSpec((1, kdim // steps, n), lambda i: (0, i, 0)))
            out_shape.append(jax.ShapeDtypeStruct((1, kdim, n), BF))
    return pl.pallas_call(
        functools.partial(_out_kernel, last=last, tiles_per_seq=tiles_per_seq, nseq=nseq, lseq=lseq,
                          convert=convert),
        grid=(steps,),
        in_specs=in_specs, out_specs=out_specs, out_shape=out_shape,
        compiler_params=_params(1), name="out",
    )(*args)


def kernel(x_prompt, x_sample, c_prompt, c_sample, state_C, state_n, state_m, state_pool,
           norm_w, w_ada, b_ada, w_in, b_if, head_norm_w, w_pool_mix, pool_scale,
           w_branch_a, w_branch_b, w_out, final_norm_w):
    depth = w_in.shape[0]
    bp, lp, _ = x_prompt.shape
    bs, ls, _ = x_sample.shape
    tp = 256
    t_mlstm = 4 * tp
    assert bp <= SUBLANES and bs == SUBLANES and lp % t_mlstm == 0 and ls >= POOL_HIST and ls % SUBLANES == 0

    w_in_t = jnp.swapaxes(w_in, 1, 2)
    w_rest, wif = _cast_pool_side(w_in_t)
    wts = dict(w_rest=w_rest)
    first_casts = dict(w5a=(w_in_t, 0, N_QKVO), w5b=(w_in_t, N_QKVO, D_MODEL),
                       wa=(w_branch_a, 0, w_branch_a.shape[1]), wb=(w_branch_b, 0, w_branch_b.shape[1]),
                       wo=(w_out, 0, w_out.shape[1]))

    c_all = jnp.zeros((MOD_ROWS, D_MODEL), F32).at[:bp].set(c_prompt).at[SUBLANES:].set(c_sample)
    mod = _modulation(c_all, w_ada, b_ada)
    mod_t = mod.reshape(depth, MOD_ROWS, 3, D_MODEL).transpose(0, 2, 1, 3)

    nw_all = jnp.concatenate([norm_w, final_norm_w[None, :]], axis=0).reshape(depth + 1, 1, D_MODEL)
    hnw = head_norm_w.reshape(depth, 1, D_MODEL)
    ps = pool_scale.reshape(depth, 1, D_B)
    st_n = state_n.reshape(depth, bs, N_HEADS, 1, DH)
    st_m = jnp.broadcast_to(state_m[..., None, None], (depth, bs, N_HEADS, 1, LANES))
    hist0 = jnp.pad(state_pool, ((0, 0), (0, 0), (HIST_ROWS - POOL_HIST, 0), (0, 0)))

    xp = x_prompt.reshape(bp * lp, D_MODEL)
    xs = x_sample.reshape(bs * ls, D_MODEL)
    prompt = dict(group=0, tiles_per_seq=lp // tp, t=tp, nseq=1, lseq=tp)
    sample = dict(group=1, tiles_per_seq=1, t=bs * ls, nseq=bs, lseq=ls)
    hp = gp = hs = gs = None

    outs = {k: [] for k in ("np", "mp", "pp", "ns", "ms", "ps")}
    c1 = c2 = None
    for l in range(depth):
        last = l == depth - 1
        nxt = {}
        res = _pool(xp if l == 0 else hp, wts["w_rest"], w_pool_mix, ps, None,
                    (nw_all, mod_t, wif, 0) if l == 0 else None, w_in_t,
                    layer=l, groups=bp, tiles=lp // tp, lseq=tp, nseq=1, pos0=0, convert=not last,
                    side_casts=tuple(first_casts.values()) if l == 0 else ())
        bb_p, sga_p, sgb_p, hist_p = res[:4]
        if l == 0:
            hp, gp = res[4:6]
            wts.update(zip(first_casts, res[len(res) - len(first_casts):]))
            res = res[:len(res) - len(first_casts)]
        if not last:
            nxt["w_rest"] = res[-1]
        res = _mlstm_prompt(hp, gp, wts, b_if, hnw, c1, w_in_t, depth=depth, layer=l, batch=bp,
                            seq_rows=lp, t=t_mlstm, chunk=tp)
        a_p, c1, n1, m1 = res[:4]
        if not last:
            nxt["w5a"], nxt["w5b"] = res[4:]
        res = _out(xp, a_p, bb_p, sga_p, sgb_p, mod_t, wts, nw_all, wif, (w_branch_a, w_branch_b, w_out),
                   layer=l, last=last, convert=not last, **prompt)
        if last:
            (yp,) = res
        else:
            xp, hp, gp, nxt["wa"], nxt["wb"], nxt["wo"] = res
        outs["np"].append(n1[:, :, 0, :])
        outs["mp"].append(m1[:, :, 0, 0])
        outs["pp"].append(hist_p[:, HIST_ROWS - POOL_HIST:, :])

        res = _pool_stream(xs if l == 0 else hs, wts["w_rest"], w_pool_mix, ps, hist0,
                           (nw_all, mod_t, wif, 1) if l == 0 else None, layer=l,
                           lseq=ls, nseq=bs, pos0=PAST_LEN)
        bb_s, sga_s, sgb_s, hist_s = res[:4]
        if l == 0:
            hs, gs = res[4:]
        a_s, c2, n2, m2 = _mlstm_seqs(hs, gs, wts, b_if, hnw, state_C, st_n, st_m, c2, depth=depth, layer=l,
                                      batch=bs, lseq=ls, per_step=4)
        res = _out(xs, a_s, bb_s, sga_s, sgb_s, mod_t, wts, nw_all, wif, None, layer=l, last=last,
                   convert=False, **sample)
        if last:
            (ys,) = res
        else:
            xs, hs, gs = res
        outs["ns"].append(n2[:, :, 0, :])
        outs["ms"].append(m2[:, :, 0, 0])
        outs["ps"].append(hist_s[:, HIST_ROWS - POOL_HIST:, :])
        wts = nxt

    st = {k: jnp.stack(v) for k, v in outs.items()}
    return (yp.reshape(bp, lp, D_MODEL), ys.reshape(bs, ls, D_MODEL),
            c1, st["np"], st["mp"], st["pp"],
            c2, st["ns"], st["ms"], st["ps"])
```

```python
import functools

import jax
import jax.numpy as jnp
from jax import lax
from jax.experimental import pallas as pl
from jax.experimental.pallas import tpu as pltpu

D_MODEL = 2048
N_HEADS = 4
DH = D_MODEL // N_HEADS
D_B = D_MODEL // 2
POOL_WINDOWS = (2, 4, 8, 16)
POOL_GROUP = D_B // len(POOL_WINDOWS)
POOL_HIST = 15
HIST_ROWS = 16
PAST_LEN = 2048
EPS = 1e-6
NEG = -1e30
N_QKVO = 4 * D_MODEL
N_QKVOZ = 5 * D_MODEL
N_IF = 2 * N_HEADS
N_REST = 2 * D_B + 2 * D_MODEL

LANES = 128
SUBLANES = 8
VMEM_LIMIT = 60 * 1024 * 1024
MOD_ROWS = 2 * SUBLANES

BF = jnp.bfloat16
F32 = jnp.float32

_dot = functools.partial(jnp.dot, preferred_element_type=F32)


def _dot_nt(a, b):
    return lax.dot_general(a, b, (((1,), (1,)), ((), ())), preferred_element_type=F32)


def _dot_tn(a, b):
    return lax.dot_general(a, b, (((0,), (0,)), ((), ())), preferred_element_type=F32)


def _params(n_axes):
    return pltpu.CompilerParams(dimension_semantics=("arbitrary",) * n_axes,
                                vmem_limit_bytes=VMEM_LIMIT)


def _layer_spec(shape, layer):
    nd = len(shape)
    return pl.BlockSpec((1,) + tuple(shape[1:]), lambda *_: (layer,) + (0,) * (nd - 1),
                        pipeline_mode=pl.Buffered(1))


def _silu(x):
    return x * jax.nn.sigmoid(x)


def _log_sigmoid(x):
    return jnp.minimum(x, 0.0) - jnp.log(1.0 + jnp.exp(-jnp.abs(x)))


def _shifted_rows(a_rows, next_rows):
    return jnp.concatenate([a_rows[N_IF:, :], next_rows], axis=0)


def _shift_specs(layer, nb, block_of_step):
    first = N_QKVOZ // nb
    return [pl.BlockSpec((1, nb, D_MODEL), lambda *g: (layer, first + block_of_step(*g), 0)),
            pl.BlockSpec((1, N_IF, D_MODEL),
                         lambda *g: (layer, (first + block_of_step(*g) + 1) * (nb // N_IF), 0))]


def _cast_shift_kernel(a_ref, b_ref, o_ref):
    o_ref[0] = _shifted_rows(a_ref[0], b_ref[0]).astype(BF)


def _cast_gate_kernel(w_ref, o_ref):
    rows = jnp.concatenate([w_ref[0], jnp.zeros((LANES - N_IF, D_MODEL), F32)], axis=0)
    o_ref[0] = rows.T.astype(BF)


def _cast_pool_side(w_in_t):
    depth = w_in_t.shape[0]
    nb = 1024
    w_rest = pl.pallas_call(
        _cast_shift_kernel, grid=(N_REST // nb,),
        in_specs=_shift_specs(0, nb, lambda j: j),
        out_specs=pl.BlockSpec((1, nb, D_MODEL), lambda j: (0, j, 0)),
        out_shape=jax.ShapeDtypeStruct((1, N_REST, D_MODEL), BF),
        compiler_params=_params(1), name="cast_rest")(w_in_t, w_in_t)
    wif = pl.pallas_call(
        _cast_gate_kernel, grid=(depth,),
        in_specs=[pl.BlockSpec((1, N_IF, D_MODEL), lambda l: (l, N_QKVOZ // N_IF, 0))],
        out_specs=pl.BlockSpec((1, D_MODEL, LANES), lambda l: (l, 0, 0)),
        out_shape=jax.ShapeDtypeStruct((depth, D_MODEL, LANES), BF),
        compiler_params=_params(1), name="cast_gates")(w_in_t)
    return w_rest, wif


def _mod_kernel(c_ref, w_ref, b_ref, o_ref):
    @pl.when(pl.program_id(1) == 0)
    def _init():
        o_ref[0] = jnp.broadcast_to(b_ref[0], o_ref.shape[1:])

    o_ref[0] += _dot(_silu(c_ref[...]).astype(BF), w_ref[0].astype(BF))


def _modulation(c_all, w_ada, b_ada):
    depth, kdim, n = w_ada.shape
    tk = 512
    return pl.pallas_call(
        _mod_kernel,
        grid=(depth, kdim // tk),
        in_specs=[pl.BlockSpec((MOD_ROWS, tk), lambda l, k: (0, k)),
                  pl.BlockSpec((1, tk, n), lambda l, k: (l, k, 0)),
                  pl.BlockSpec((1, 1, n), lambda l, k: (l, 0, 0))],
        out_specs=pl.BlockSpec((1, MOD_ROWS, n), lambda l, k: (l, 0, 0)),
        out_shape=jax.ShapeDtypeStruct((depth, MOD_ROWS, n), F32),
        compiler_params=_params(2), name="modulation",
    )(c_all, w_ada, b_ada.reshape(depth, 1, n))


def _mod_vec(ref, seq, nseq, lseq):
    if nseq == 1:
        return ref[0, 0, pl.ds(seq, 1), :]
    blk = ref[0, 0]
    return jnp.concatenate([jnp.broadcast_to(blk[s:s + 1, :], (lseq, D_MODEL)) for s in range(nseq)], axis=0)


def _mod_spec(layer, kind, group):
    return pl.BlockSpec((1, 1, SUBLANES, D_MODEL), lambda *_: (layer, kind, group, 0))


def _norm_mod(y, nw, scl, shift):
    r = lax.rsqrt(jnp.mean(y * y, axis=-1, keepdims=True) + EPS)
    return (y * r * nw) * (1.0 + scl) + shift


def _gate_columns(g, hd, b_i, b_f):
    lane = lax.broadcasted_iota(jnp.int32, g.shape, 1)
    ig = jnp.sum(jnp.where(lane == hd, g, 0.0), axis=1, keepdims=True) + b_i
    fg = jnp.sum(jnp.where(lane == hd + N_HEADS, g, 0.0), axis=1, keepdims=True) + b_f
    return ig, _log_sigmoid(fg)


def _recurrence(q, k, v, ig_c, lf_c, c_old, n_old, m0):
    t = q.shape[0]
    qb = q.astype(BF)
    vb = v.astype(BF)
    row = lax.broadcasted_iota(jnp.int32, (t, t), 0)
    col = lax.broadcasted_iota(jnp.int32, (t, t), 1)
    causal = col <= row
    eye = col == row
    ig_r = jnp.sum(jnp.where(eye, ig_c, 0.0), axis=0, keepdims=True)
    lf_r = jnp.sum(jnp.where(eye, lf_c, 0.0), axis=0, keepdims=True)
    b_c = jnp.sum(jnp.where(causal, lf_r, 0.0), axis=1, keepdims=True)
    b_r = jnp.sum(jnp.where(row <= col, lf_c, 0.0), axis=0, keepdims=True)
    g_r = ig_r - b_r
    g_c = ig_c - b_c
    big_m = jnp.maximum(jnp.max(jnp.where(causal, g_r, NEG), axis=1, keepdims=True), m0)
    w_intra = jnp.exp(jnp.where(causal, g_r - big_m, NEG))
    a_inter = jnp.exp(m0 - big_m)

    s = _dot_nt(qb, k.astype(BF)) * w_intra
    num = a_inter * _dot(qb, c_old.astype(BF)) + _dot(s.astype(BF), vb)
    den = a_inter * jnp.sum(q * n_old, axis=1, keepdims=True) + jnp.sum(s, axis=1, keepdims=True)
    h = num * (1.0 / jnp.maximum(jnp.abs(den), jnp.exp(-(b_c + big_m))))

    m_last = big_m[t - 1:t, :]
    kw = k * jnp.exp(g_c - m_last)
    decay = jnp.exp(m0 - m_last)
    c_new = decay * c_old + _dot_tn(kw.astype(BF), vb)
    n_new = decay * n_old + jnp.sum(kw, axis=0, keepdims=True)
    return h, c_new, n_new, b_c[t - 1:t, :] + m_last


def _head_out(o, z, h, hnw):
    og = jax.nn.sigmoid(o) * h
    mu = jnp.mean(og, axis=1, keepdims=True)
    dev = og - mu
    var = jnp.mean(dev * dev, axis=1, keepdims=True)
    return (_silu(z) * (dev * lax.rsqrt(var + EPS) * hnw)).astype(BF)


def _mlstm_kernel(*refs, layer, chunk, aliased, convert):
    bif_ref, h_ref, g_ref, wq_ref, wk_ref, wv_ref, wo_ref, wz_ref, hnw_ref = refs[:9]
    refs = refs[9 + aliased:]
    if convert:
        fa_ref, fb_ref, a_ref, c_ref, n_ref, m_ref, na_ref, nb_ref = refs
        na_ref[...] = fa_ref[...].astype(BF)
        nb_ref[...] = fb_ref[...].astype(BF)
    else:
        a_ref, c_ref, n_ref, m_ref = refs
    hd = pl.program_id(0)

    @pl.when(pl.program_id(2) == 0)
    def _init():
        c_ref[0, 0, 0] = jnp.zeros((DH, DH), F32)
        n_ref[0, 0] = jnp.zeros((1, DH), F32)
        m_ref[0, 0] = jnp.zeros((1, LANES), F32)

    hb = h_ref[...]
    q = _dot_nt(hb, wq_ref[0])
    k = _dot_nt(hb, wk_ref[0]) * (DH ** -0.5)
    v = _dot_nt(hb, wv_ref[0])
    ig_c, lf_c = _gate_columns(g_ref[...], hd, bif_ref[layer, hd], bif_ref[layer, N_HEADS + hd])
    c, n, m = c_ref[0, 0, 0], n_ref[0, 0], m_ref[0, 0][:, 0:1]
    hs = []
    for j in range(hb.shape[0] // chunk):
        r = slice(j * chunk, (j + 1) * chunk)
        h, c, n, m = _recurrence(q[r], k[r], v[r], ig_c[r], lf_c[r], c, n, m)
        hs.append(h)
    c_ref[0, 0, 0] = c
    n_ref[0, 0] = n
    m_ref[0, 0] = jnp.broadcast_to(m, (1, LANES))
    a_ref[...] = _head_out(_dot_nt(hb, wo_ref[0]), _dot_nt(hb, wz_ref[0]), jnp.concatenate(hs, axis=0), hnw_ref[0])


def _w5_specs(head_of):
    qkvo = [pl.BlockSpec((1, DH, D_MODEL), lambda *g, grp=grp: (0, grp * N_HEADS + head_of(*g), 0))
            for grp in range(4)]
    return qkvo + [pl.BlockSpec((1, DH, D_MODEL), lambda *g: (0, head_of(*g), 0))]


def _state_out(depth, batch, layer, c_stack, index_map, n_inputs, per_step=1):
    specs = [pl.BlockSpec((1, per_step, 1, DH, DH), lambda *g: (layer,) + index_map(*g)),
             pl.BlockSpec((per_step, 1, 1, DH), index_map),
             pl.BlockSpec((per_step, 1, 1, LANES), index_map)]
    shapes = [jax.ShapeDtypeStruct((depth, batch, N_HEADS, DH, DH), F32),
              jax.ShapeDtypeStruct((batch, N_HEADS, 1, DH), F32),
              jax.ShapeDtypeStruct((batch, N_HEADS, 1, LANES), F32)]
    if c_stack is None:
        return specs, shapes, [], [], {}
    return specs, shapes, [pl.BlockSpec(memory_space=pl.ANY)], [c_stack], {n_inputs: 1}


def _mlstm_prompt(h2, gates, wts, b_if, hnw, c_stack, w_in_t, *, depth, layer, batch, seq_rows, t, chunk):
    nck = seq_rows // t
    rows = batch * seq_rows
    steps = N_HEADS * batch * nck
    convert = layer + 1 < depth
    in_specs = ([pl.BlockSpec(memory_space=pltpu.SMEM),
                 pl.BlockSpec((t, D_MODEL), lambda hd, b, c: (b * nck + c, 0)),
                 pl.BlockSpec((t, LANES), lambda hd, b, c: (b * nck + c, 0))]
                + _w5_specs(lambda hd, b, c: hd)
                + [pl.BlockSpec((1, 1, DH), lambda hd, b, c: (layer, 0, hd))])
    args = [b_if, h2, gates, wts["w5a"], wts["w5a"], wts["w5a"], wts["w5a"], wts["w5b"], hnw]
    st_specs, st_shapes, extra_specs, extra_args, aliases = _state_out(
        depth, batch, layer, c_stack, lambda hd, b, c: (b, hd, 0, 0), len(in_specs))
    in_specs += extra_specs
    args += extra_args
    out_specs = [pl.BlockSpec((t, DH), lambda hd, b, c: (b * nck + c, hd))] + st_specs
    out_shape = [jax.ShapeDtypeStruct((rows, D_MODEL), BF)] + st_shapes
    if convert:
        assert N_QKVO % (2 * SUBLANES * steps) == 0 and D_MODEL % (2 * SUBLANES * steps) == 0
        ra, rb = N_QKVO // steps, D_MODEL // steps
        step = lambda hd, b, c: (hd * batch + b) * nck + c
        in_specs += [pl.BlockSpec((1, ra, D_MODEL), lambda *g: (layer + 1, step(*g), 0)),
                     pl.BlockSpec((1, rb, D_MODEL), lambda *g: (layer + 1, N_QKVO // rb + step(*g), 0))]
        args += [w_in_t, w_in_t]
        out_specs += [pl.BlockSpec((1, ra, D_MODEL), lambda *g: (0, step(*g), 0)),
                      pl.BlockSpec((1, rb, D_MODEL), lambda *g: (0, step(*g), 0))]
        out_shape += [jax.ShapeDtypeStruct((1, N_QKVO, D_MODEL), BF),
                      jax.ShapeDtypeStruct((1, D_MODEL, D_MODEL), BF)]
    return pl.pallas_call(
        functools.partial(_mlstm_kernel, layer=layer, chunk=chunk, aliased=c_stack is not None, convert=convert),
        grid=(N_HEADS, batch, nck),
        in_specs=in_specs, out_specs=out_specs, out_shape=out_shape,
        input_output_aliases=aliases,
        compiler_params=_params(3), name="mlstm",
    )(*args)


def _mlstm_seqs_kernel(bif_ref, h_ref, g_ref, wq_ref, wk_ref, wv_ref, wo_ref, wz_ref, hnw_ref,
                       c0_ref, n0_ref, m0_ref, *rest, layer, lseq, per_step):
    a_ref, c_ref, n_ref, m_ref, q_s, k_s, v_s, h_s, ig_s, lf_s = rest[-10:]
    hd = pl.program_id(0)
    b = pl.program_id(1)

    @pl.when(b == 0)
    def _project():
        hb = h_ref[...]
        q_s[...] = _dot_nt(hb, wq_ref[0])
        k_s[...] = _dot_nt(hb, wk_ref[0]) * (DH ** -0.5)
        v_s[...] = _dot_nt(hb, wv_ref[0])
        ig_c, lf_c = _gate_columns(g_ref[...], hd, bif_ref[layer, hd], bif_ref[layer, N_HEADS + hd])
        ig_s[...] = ig_c
        lf_s[...] = lf_c

    for j in range(per_step):
        rows = pl.ds(pl.multiple_of((b * per_step + j) * lseq, lseq), lseq)
        h, c_new, n_new, m_new = _recurrence(q_s[rows, :], k_s[rows, :], v_s[rows, :], ig_s[rows, :],
                                             lf_s[rows, :], c0_ref[0, j, 0], n0_ref[0, j, 0],
                                             m0_ref[0, j, 0][:, 0:1])
        c_ref[0, j, 0] = c_new
        n_ref[j, 0] = n_new
        m_ref[j, 0] = jnp.broadcast_to(m_new, (1, LANES))
        h_s[rows, :] = h

    @pl.when(b == pl.num_programs(1) - 1)
    def _finish():
        hb = h_ref[...]
        a_ref[...] = _head_out(_dot_nt(hb, wo_ref[0]), _dot_nt(hb, wz_ref[0]), h_s[...], hnw_ref[0])


def _mlstm_seqs(h2, gates, wts, b_if, hnw, state_c, state_n, state_m, c_stack, *, depth, layer, batch, lseq,
                per_step):
    rows = batch * lseq
    in_specs = ([pl.BlockSpec(memory_space=pltpu.SMEM),
                 pl.BlockSpec((rows, D_MODEL), lambda hd, b: (0, 0)),
                 pl.BlockSpec((rows, LANES), lambda hd, b: (0, 0))]
                + _w5_specs(lambda hd, b: hd)
                + [pl.BlockSpec((1, 1, DH), lambda hd, b: (layer, 0, hd)),
                   pl.BlockSpec((1, per_step, 1, DH, DH), lambda hd, b: (layer, b, hd, 0, 0)),
                   pl.BlockSpec((1, per_step, 1, 1, DH), lambda hd, b: (layer, b, hd, 0, 0)),
                   pl.BlockSpec((1, per_step, 1, 1, LANES), lambda hd, b: (layer, b, hd, 0, 0))])
    st_specs, st_shapes, extra_specs, extra_args, aliases = _state_out(
        depth, batch, layer, c_stack, lambda hd, b: (b, hd, 0, 0), len(in_specs), per_step)
    return pl.pallas_call(
        functools.partial(_mlstm_seqs_kernel, layer=layer, lseq=lseq, per_step=per_step),
        grid=(N_HEADS, batch // per_step),
        in_specs=in_specs + extra_specs,
        out_specs=[pl.BlockSpec((rows, DH), lambda hd, b: (0, hd))] + st_specs,
        out_shape=[jax.ShapeDtypeStruct((rows, D_MODEL), BF)] + st_shapes,
        input_output_aliases=aliases,
        scratch_shapes=[pltpu.VMEM((rows, DH), F32)] * 4 + [pltpu.VMEM((rows, 1), F32)] * 2,
        compiler_params=_params(2), name="mlstm_seqs",
    )(b_if, h2, gates, wts["w5a"], wts["w5a"], wts["w5a"], wts["w5a"], wts["w5b"], hnw,
      state_c, state_n, state_m, *extra_args)


def _pool_kernel(*refs, lseq, nseq, pos0, first, has_state, convert, n_casts):
    src_ref, w_ref, wmix_ref, ps_ref = refs[:4]
    refs = refs[4:]
    if first:
        nw_ref, shift_ref, scl_ref, wif_ref = refs[:4]
        refs = refs[4:]
    if has_state:
        hist0_ref, refs = refs[0], refs[1:]
    if convert:
        ra_ref, rb_ref = refs[:2]
        refs = refs[2:]
    cast_src, refs = refs[:n_casts], refs[n_casts:]
    bb_ref, sga_ref, sgb_ref, hist_ref = refs[:4]
    uext = refs[-1]
    refs = refs[4:-1]
    for src, dst in zip(cast_src, refs[len(refs) - n_casts:]):
        dst[...] = src[...].astype(BF)
    refs = refs[:len(refs) - n_casts]
    if convert:
        refs[-1][0] = _shifted_rows(ra_ref[0], rb_ref[0]).astype(BF)
    i = pl.program_id(1)
    seg = HIST_ROWS + lseq

    @pl.when(i == 0)
    def _first():
        for sq in range(nseq):
            uext[sq * seg:sq * seg + HIST_ROWS, :] = (hist0_ref[0, sq] if has_state
                                                       else jnp.zeros((HIST_ROWS, D_B), F32))

    @pl.when(i > 0)
    def _carry():
        uext[0:HIST_ROWS, :] = uext[lseq:seg, :]

    if first:
        seq = pl.program_id(0)
        hb = _norm_mod(src_ref[...], nw_ref[0], _mod_vec(scl_ref, seq, nseq, lseq),
                       _mod_vec(shift_ref, seq, nseq, lseq)).astype(BF)
        refs[0][...] = hb
        refs[1][...] = _dot(hb, wif_ref[0])
    else:
        hb = src_ref[...]
    u = _dot_nt(hb, w_ref[0, 0:D_B, :])
    for sq in range(nseq):
        uext[sq * seg + HIST_ROWS:(sq + 1) * seg, :] = u[sq * lseq:(sq + 1) * lseq, :]
    zb = _dot_nt(hb, w_ref[0, D_B:2 * D_B, :])
    sga_ref[...] = jax.nn.sigmoid(_dot_nt(hb, w_ref[0, 2 * D_B:2 * D_B + D_MODEL, :]))
    sgb_ref[...] = jax.nn.sigmoid(_dot_nt(hb, w_ref[0, 2 * D_B + D_MODEL:, :]))

    in_seq = lax.broadcasted_iota(jnp.int32, (lseq, 1), 0)
    pos = jnp.concatenate([in_seq] * nseq, axis=0) + (i * lseq + pos0 + 1)
    for g, w in enumerate(POOL_WINDOWS):
        sl = slice(g * POOL_GROUP, (g + 1) * POOL_GROUP)
        acc = uext[:, sl]
        shift = 1
        while shift < w:
            acc = acc + pltpu.roll(acc, shift, axis=0)
            shift *= 2
        win = jnp.concatenate([acc[sq * seg + HIST_ROWS:(sq + 1) * seg, :] for sq in range(nseq)], axis=0)
        pooled = win / jnp.minimum(pos, w).astype(F32)
        dlt = (pooled - u[:, sl]).astype(BF)
        mixed = _dot(dlt, wmix_ref[0, g].astype(BF)) * ps_ref[0][:, sl]
        bb_ref[:, sl] = (_silu(zb[:, sl]) * mixed).astype(BF)

    @pl.when(i == pl.num_programs(1) - 1)
    def _hist():
        for sq in range(nseq):
            hist_ref[sq] = uext[sq * seg + lseq:(sq + 1) * seg, :]


def _pool(src, w_rest, wmix, ps, hist0, first_inputs, w_in_t, *, layer, groups, tiles, lseq, nseq, pos0, convert,
          side_casts=()):
    t = lseq * nseq
    rows = groups * tiles * t
    has_state = hist0 is not None
    first = first_inputs is not None
    row_spec = lambda width: pl.BlockSpec((t, width), lambda b, i: (b * tiles + i, 0))
    in_specs = [row_spec(D_MODEL), _layer_spec(w_rest.shape, 0), _layer_spec(wmix.shape, layer),
                _layer_spec(ps.shape, layer)]
    args = [src, w_rest, wmix, ps]
    if first:
        nw_all, mod_t, wif, group = first_inputs
        in_specs += [pl.BlockSpec((1, 1, D_MODEL), lambda b, i: (layer, 0, 0)),
                     _mod_spec(layer, 0, group), _mod_spec(layer, 1, group),
                     pl.BlockSpec((1, D_MODEL, LANES), lambda b, i: (layer, 0, 0))]
        args += [nw_all, mod_t, mod_t, wif]
    if has_state:
        in_specs.append(pl.BlockSpec((1, nseq, HIST_ROWS, D_B), lambda b, i: (layer, b, 0, 0)))
        args.append(hist0)
    out_specs = [row_spec(D_B), row_spec(D_MODEL), row_spec(D_MODEL),
                 pl.BlockSpec((nseq, HIST_ROWS, D_B), lambda b, i: (b, 0, 0))]
    out_shape = [jax.ShapeDtypeStruct((rows, D_B), BF),
                 jax.ShapeDtypeStruct((rows, D_MODEL), F32),
                 jax.ShapeDtypeStruct((rows, D_MODEL), F32),
                 jax.ShapeDtypeStruct((groups * nseq, HIST_ROWS, D_B), F32)]
    if first:
        out_specs += [row_spec(D_MODEL), row_spec(LANES)]
        out_shape += [jax.ShapeDtypeStruct((rows, D_MODEL), BF), jax.ShapeDtypeStruct((rows, LANES), F32)]
    if convert:
        n_blk = N_REST // LANES
        assert groups * tiles >= n_blk
        blk = lambda b, i: jnp.minimum(b * tiles + i, n_blk - 1)
        in_specs += _shift_specs(layer + 1, LANES, blk)
        args += [w_in_t, w_in_t]
        out_specs.append(pl.BlockSpec((1, LANES, D_MODEL), lambda b, i: (0, blk(b, i), 0)))
        out_shape.append(jax.ShapeDtypeStruct((1, N_REST, D_MODEL), BF))
    steps = groups * tiles
    cast_in_specs = []
    for w, first_row, nrows in side_casts:
        assert nrows % (2 * SUBLANES * steps) == 0 and first_row % (nrows // steps) == 0
        rps, ncols = nrows // steps, w.shape[2]
        cast_in_specs.append(pl.BlockSpec((1, rps, ncols),
                                          lambda b, i, base=first_row // rps: (layer, base + b * tiles + i, 0)))
        out_specs.append(pl.BlockSpec((1, rps, ncols), lambda b, i: (0, b * tiles + i, 0)))
        out_shape.append(jax.ShapeDtypeStruct((1, nrows, ncols), BF))
    in_specs += cast_in_specs
    args += [w for w, _, _ in side_casts]
    return pl.pallas_call(
        functools.partial(_pool_kernel, lseq=lseq, nseq=nseq, pos0=pos0, first=first, has_state=has_state,
                          convert=convert, n_casts=len(side_casts)),
        grid=(groups, tiles),
        in_specs=in_specs, out_specs=out_specs, out_shape=out_shape,
        scratch_shapes=[pltpu.VMEM((nseq * (HIST_ROWS + lseq), D_B), F32)],
        compiler_params=_params(2), name="pool",
    )(*args)


def _out_kernel(*refs, last, tiles_per_seq, nseq, lseq, convert):
    x_ref, a_ref, bb_ref, sga_ref, sgb_ref, gate_ref, wa_ref, wb_ref, wo_ref, nw_ref = refs[:10]
    refs = refs[10:]
    if not last:
        shift_ref, scl_ref, wif_ref = refs[:3]
        refs = refs[3:]
    if convert:
        fa_ref, fb_ref, fo_ref = refs[:3]
        refs = refs[3:]
        na_ref, nb_ref, no_ref = refs[-3:]
        na_ref[...] = fa_ref[...].astype(BF)
        nb_ref[...] = fb_ref[...].astype(BF)
        no_ref[...] = fo_ref[...].astype(BF)
    seq = pl.program_id(0) // tiles_per_seq
    br_a = _dot(a_ref[...], wa_ref[0])
    br_b = _dot(bb_ref[...], wb_ref[0])
    merged = (sga_ref[...] * br_a + sgb_ref[...] * br_b).astype(BF)
    y = x_ref[...] + _mod_vec(gate_ref, seq, nseq, lseq) * _dot(merged, wo_ref[0])
    if last:
        r = lax.rsqrt(jnp.mean(y * y, axis=-1, keepdims=True) + EPS)
        refs[0][...] = y * r * nw_ref[0]
    else:
        y_ref, hn_ref, g_ref = refs[:3]
        y_ref[...] = y
        hn = _norm_mod(y, nw_ref[0], _mod_vec(scl_ref, seq, nseq, lseq),
                       _mod_vec(shift_ref, seq, nseq, lseq)).astype(BF)
        hn_ref[...] = hn
        g_ref[...] = _dot(hn, wif_ref[0])


def _out(x2, a2, bb2, sga, sgb, mod_t, wts, nw_all, wif, f32_weights, *, layer, group, tiles_per_seq, t, nseq,
         lseq, last, convert):
    rows = x2.shape[0]
    steps = rows // t
    row_spec = lambda width: pl.BlockSpec((t, width), lambda i: (i, 0))
    in_specs = [row_spec(D_MODEL), row_spec(D_MODEL), row_spec(D_B), row_spec(D_MODEL), row_spec(D_MODEL),
                _mod_spec(layer, 2, group),
                _layer_spec(wts["wa"].shape, 0), _layer_spec(wts["wb"].shape, 0), _layer_spec(wts["wo"].shape, 0),
                pl.BlockSpec((1, 1, D_MODEL), lambda i: (layer + 1, 0, 0))]
    args = [x2, a2, bb2, sga, sgb, mod_t, wts["wa"], wts["wb"], wts["wo"], nw_all]
    if last:
        out_specs = [row_spec(D_MODEL)]
        out_shape = [jax.ShapeDtypeStruct((rows, D_MODEL), F32)]
    else:
        in_specs += [_mod_spec(layer + 1, 0, group), _mod_spec(layer + 1, 1, group),
                     pl.BlockSpec((1, D_MODEL, LANES), lambda i: (layer + 1, 0, 0))]
        args += [mod_t, mod_t, wif]
        out_specs = [row_spec(D_MODEL), row_spec(D_MODEL), row_spec(LANES)]
        out_shape = [jax.ShapeDtypeStruct((rows, D_MODEL), F32),
                     jax.ShapeDtypeStruct((rows, D_MODEL), BF),
                     jax.ShapeDtypeStruct((rows, LANES), F32)]
    if convert:
        for w in f32_weights:
            _, kdim, n = w.shape
            assert kdim % (2 * SUBLANES * steps) == 0
            in_specs.append(pl.BlockSpec((1, kdim // steps, n), lambda i: (layer + 1, i, 0)))
            args.append(w)
            out_specs.append(pl.BlockSpec((1, kdim // steps, n), lambda i: (0, i, 0)))
            out_shape.append(jax.ShapeDtypeStruct((1, kdim, n), BF))
    return pl.pallas_call(
        functools.partial(_out_kernel, last=last, tiles_per_seq=tiles_per_seq, nseq=nseq, lseq=lseq,
                          convert=convert),
        grid=(steps,),
        in_specs=in_specs, out_specs=out_specs, out_shape=out_shape,
        compiler_params=_params(1), name="out",
    )(*args)


def kernel(x_prompt, x_sample, c_prompt, c_sample, state_C, state_n, state_m, state_pool,
           norm_w, w_ada, b_ada, w_in, b_if, head_norm_w, w_pool_mix, pool_scale,
           w_branch_a, w_branch_b, w_out, final_norm_w):
    depth = w_in.shape[0]
    bp, lp, _ = x_prompt.shape
    bs, ls, _ = x_sample.shape
    tp = 256
    t_mlstm = 4 * tp
    assert bp <= SUBLANES and bs == SUBLANES and lp % t_mlstm == 0 and ls >= POOL_HIST and ls % SUBLANES == 0

    w_in_t = jnp.swapaxes(w_in, 1, 2)
    w_rest, wif = _cast_pool_side(w_in_t)
    wts = dict(w_rest=w_rest)
    first_casts = dict(w5a=(w_in_t, 0, N_QKVO), w5b=(w_in_t, N_QKVO, D_MODEL),
                       wa=(w_branch_a, 0, w_branch_a.shape[1]), wb=(w_branch_b, 0, w_branch_b.shape[1]),
                       wo=(w_out, 0, w_out.shape[1]))

    c_all = jnp.zeros((MOD_ROWS, D_MODEL), F32).at[:bp].set(c_prompt).at[SUBLANES:].set(c_sample)
    mod = _modulation(c_all, w_ada, b_ada)
    mod_t = mod.reshape(depth, MOD_ROWS, 3, D_MODEL).transpose(0, 2, 1, 3)

    nw_all = jnp.concatenate([norm_w, final_norm_w[None, :]], axis=0).reshape(depth + 1, 1, D_MODEL)
    hnw = head_norm_w.reshape(depth, 1, D_MODEL)
    ps = pool_scale.reshape(depth, 1, D_B)
    st_n = state_n.reshape(depth, bs, N_HEADS, 1, DH)
    st_m = jnp.broadcast_to(state_m[..., None, None], (depth, bs, N_HEADS, 1, LANES))
    hist0 = jnp.pad(state_pool, ((0, 0), (0, 0), (HIST_ROWS - POOL_HIST, 0), (0, 0)))

    xp = x_prompt.reshape(bp * lp, D_MODEL)
    xs = x_sample.reshape(bs * ls, D_MODEL)
    prompt = dict(group=0, tiles_per_seq=lp // tp, t=tp, nseq=1, lseq=tp)
    sample = dict(group=1, tiles_per_seq=1, t=bs * ls, nseq=bs, lseq=ls)
    hp = gp = hs = gs = None

    outs = {k: [] for k in ("np", "mp", "pp", "ns", "ms", "ps")}
    c1 = c2 = None
    for l in range(depth):
        last = l == depth - 1
        nxt = {}
        res = _pool(xp if l == 0 else hp, wts["w_rest"], w_pool_mix, ps, None,
                    (nw_all, mod_t, wif, 0) if l == 0 else None, w_in_t,
                    layer=l, groups=bp, tiles=lp // tp, lseq=tp, nseq=1, pos0=0, convert=not last,
                    side_casts=tuple(first_casts.values()) if l == 0 else ())
        bb_p, sga_p, sgb_p, hist_p = res[:4]
        if l == 0:
            hp, gp = res[4:6]
            wts.update(zip(first_casts, res[len(res) - len(first_casts):]))
            res = res[:len(res) - len(first_casts)]
        if not last:
            nxt["w_rest"] = res[-1]
        res = _mlstm_prompt(hp, gp, wts, b_if, hnw, c1, w_in_t, depth=depth, layer=l, batch=bp,
                            seq_rows=lp, t=t_mlstm, chunk=tp)
        a_p, c1, n1, m1 = res[:4]
        if not last:
            nxt["w5a"], nxt["w5b"] = res[4:]
        res = _out(xp, a_p, bb_p, sga_p, sgb_p, mod_t, wts, nw_all, wif, (w_branch_a, w_branch_b, w_out),
                   layer=l, last=last, convert=not last, **prompt)
        if last:
            (yp,) = res
        else:
            xp, hp, gp, nxt["wa"], nxt["wb"], nxt["wo"] = res
        outs["np"].append(n1[:, :, 0, :])
        outs["mp"].append(m1[:, :, 0, 0])
        outs["pp"].append(hist_p[:, HIST_ROWS - POOL_HIST:, :])

        res = _pool(xs if l == 0 else hs, wts["w_rest"], w_pool_mix, ps, hist0,
                    (nw_all, mod_t, wif, 1) if l == 0 else None, None, layer=l,
                    groups=1, tiles=1, lseq=ls, nseq=bs, pos0=PAST_LEN, convert=False)
        bb_s, sga_s, sgb_s, hist_s = res[:4]
        if l == 0:
            hs, gs = res[4:]
        a_s, c2, n2, m2 = _mlstm_seqs(hs, gs, wts, b_if, hnw, state_C, st_n, st_m, c2, depth=depth, layer=l,
                                      batch=bs, lseq=ls, per_step=2)
        res = _out(xs, a_s, bb_s, sga_s, sgb_s, mod_t, wts, nw_all, wif, None, layer=l, last=last,
                   convert=False, **sample)
        if last:
            (ys,) = res
        else:
            xs, hs, gs = res
        outs["ns"].append(n2[:, :, 0, :])
        outs["ms"].append(m2[:, :, 0, 0])
        outs["ps"].append(hist_s[:, HIST_ROWS - POOL_HIST:, :])
        wts = nxt

    st = {k: jnp.stack(v) for k, v in outs.items()}
    return (yp.reshape(bp, lp, D_MODEL), ys.reshape(bs, ls, D_MODEL),
            c1, st["np"], st["mp"], st["pp"],
            c2, st["ns"], st["ms"], st["ps"])
```
